```python
import math
import jax
import jax.numpy as jnp
from jax import lax
import numpy as np

D_MODEL = 1024
BATCH = 8
SEQ = 2048
DEPTH = 4
DEC_BATCH = 128
DEC_SEQ = 8
PAST_LEN = 16384
PAGE_SIZE = 128

N_EVEN = (DEPTH + 1) // 2
N_ODD = DEPTH // 2
H_A = 4
DK_A = 128
DV_A = 128
W_A = H_A * DV_A
H_B = 4
W_B = D_MODEL // 2
BLK_B = W_B // H_B
LRU_C = 8.0
CONV_W = 4
D_INNER_C = 2 * D_MODEL
P_C = 64
H_C = D_INNER_C // P_C
N_C = 128
G_C = 4
HPG_C = H_C // G_C
CONV_DIM_C = D_INNER_C + 2 * G_C * N_C
IN_C = D_INNER_C + CONV_DIM_C + H_C
IN_EVEN = 2 * H_A * DK_A + 2 * W_A + 2 * W_B
D_FF = -(-8 * D_MODEL // (3 * 256)) * 256
PLE_DIM = 256
CHUNK = 64
EPS = 1e-6
F32 = jnp.float32

kernel_name = 'hgrn2_rglru_mamba2_hybrid_step'


def _rmsnorm(x, g):
    xf = x.astype(F32)
    y = xf * lax.rsqrt(jnp.mean(xf * xf, axis=-1, keepdims=True) + EPS)
    return (y * g.astype(F32)).astype(x.dtype)


def _chunk_len(L):
    return L if L <= CHUNK else math.gcd(L, CHUNK)


def _to_chunks(t, c):
    B, L = t.shape[:2]
    return jnp.moveaxis(t.reshape((B, L // c, c) + t.shape[2:]), 1, 0)


def _from_chunks(t):
    nc, B, c = t.shape[:3]
    return jnp.moveaxis(t, 0, 1).reshape((B, nc * c) + t.shape[3:])


def _causal_conv(u, buf, w, b):
    L = u.shape[1]
    full = jnp.concatenate([buf.astype(u.dtype), u], axis=1)
    out = full[:, 0:L] * w[0]
    for k in range(1, CONV_W):
        out = out + full[:, k:k + L] * w[k]
    return out + b, full[:, L:]


def _hgrn2_scan(q, logf, k, v, S0):
    c = _chunk_len(q.shape[1])
    causal = jnp.tril(jnp.ones((c, c), dtype=bool))[None, :, :, None, None]

    def step(S, inp):
        qc, lfc, kc, vc = inp
        b = jnp.cumsum(lfc, axis=1)
        o = jnp.einsum('bthd,bhde->bthe', qc * jnp.exp(b), S)
        decay = jnp.exp(jnp.where(causal, b[:, :, None] - b[:, None, :], -jnp.inf))
        scores = jnp.einsum('bthd,bshd,btshd->bhts', qc, kc, decay)
        o = o + jnp.einsum('bhts,bshe->bthe', scores, vc)
        b_last = b[:, -1]
        k_dec = kc * jnp.exp(b_last[:, None] - b)
        S = jnp.exp(b_last)[..., None] * S + jnp.einsum('bshd,bshe->bhde', k_dec, vc)
        return S, o

    S, o = lax.scan(step, S0, (_to_chunks(q, c), _to_chunks(logf, c), _to_chunks(k, c), _to_chunks(v, c)))
    return _from_chunks(o), S


def _ssd_scan(xs, dt, log_a, Bm, Cm, S0):
    c = _chunk_len(xs.shape[1])
    causal = jnp.tril(jnp.ones((c, c), dtype=bool))[None, :, :, None]

    def step(S, inp):
        xc, dtc, lac, Bc, Cc = inp
        Bh = jnp.repeat(Bc, HPG_C, axis=2)
        Ch = jnp.repeat(Cc, HPG_C, axis=2)
        cum = jnp.cumsum(lac, axis=1)
        o = jnp.einsum('bthn,bhpn->bthp', Ch, S) * jnp.exp(cum)[..., None]
        decay = jnp.exp(jnp.where(causal, cum[:, :, None] - cum[:, None, :], -jnp.inf))
        scores = jnp.einsum('bthn,bshn->btsh', Ch, Bh) * decay * dtc[:, None]
        o = o + jnp.einsum('btsh,bshp->bthp', scores, xc)
        last = cum[:, -1]
        w = jnp.exp(last[:, None] - cum) * dtc
        S = jnp.exp(last)[..., None, None] * S + jnp.einsum('bsh,bshp,bshn->bhpn', w, xc, Bh)
        return S, o

    S, o = lax.scan(step, S0, (_to_chunks(xs, c), _to_chunks(dt, c), _to_chunks(log_a, c),
                               _to_chunks(Bm, c), _to_chunks(Cm, c)))
    return _from_chunks(o), S


def _rglru(u, w_a, b_a, w_x, b_x, lam, h0, fresh):
    B, L, _ = u.shape
    ub = u.reshape(B, L, H_B, BLK_B)
    r = jax.nn.sigmoid(jnp.einsum('blhi,hij->blhj', ub, w_a) + b_a).reshape(B, L, W_B)
    gi = jax.nn.sigmoid(jnp.einsum('blhi,hij->blhj', ub, w_x) + b_x).reshape(B, L, W_B)
    log_a = -LRU_C * r * jax.nn.softplus(-lam.astype(F32))
    a = jnp.exp(log_a)
    mult = jnp.sqrt(-jnp.expm1(2.0 * log_a))
    if fresh:
        mult = mult.at[:, 0].set(1.0)
    bt = mult * (gi * u)
    bt = bt.at[:, 0].add(a[:, 0] * h0)

    def combine(lhs, rhs):
        a1, b1 = lhs
        a2, b2 = rhs
        return a1 * a2, a2 * b1 + b2

    _, h = lax.associative_scan(combine, (a, bt), axis=1)
    return h, h[:, -1]


def _even_mixer(xn, S0, h0, cbuf0, fresh, lb, W, j):
    B, L, _ = xn.shape
    hk = H_A * DK_A
    proj = (xn @ W['w_even_in'][j]).astype(F32)
    q, fz, iv, g, yb, ub = jnp.split(
        proj, [hk, 2 * hk, 2 * hk + W_A, 2 * hk + 2 * W_A, 2 * hk + 2 * W_A + W_B], axis=-1)
    lb = lb.reshape(H_A, DK_A)
    fz = fz.reshape(B, L, H_A, DK_A)
    logf = jnp.logaddexp(jnp.log(lb), jnp.log1p(-lb) + jax.nn.log_sigmoid(fz))
    k = (1.0 - lb) * jax.nn.sigmoid(-fz)
    q = jax.nn.silu(q).reshape(B, L, H_A, DK_A)
    o_a, S = _hgrn2_scan(q, logf, k, iv.reshape(B, L, H_A, DV_A), S0.astype(F32))
    o_a = o_a * lax.rsqrt(jnp.mean(o_a * o_a, axis=-1, keepdims=True) + EPS)
    o_a = o_a.reshape(B, L, W_A) * W['hgrn_gnorm'][j].astype(F32) * jax.nn.silu(g)
    u, cbuf = _causal_conv(ub, cbuf0.astype(F32), W['lru_conv_w'][j], W['lru_conv_b'][j])
    h, h_last = _rglru(u, W['lru_wa'][j], W['lru_ba'][j], W['lru_wx'][j], W['lru_bx'][j],
                       W['lru_lam'][j], h0.astype(F32), fresh)
    o_b = jax.nn.gelu(yb, approximate=True) * h
    out = jnp.concatenate([o_a, o_b], axis=-1).astype(xn.dtype) @ W['w_even_out'][j]
    return out, S, h_last, cbuf


def _odd_mixer(xn, S0, cbuf0, W, j):
    B, L, _ = xn.shape
    proj = (xn @ W['ssm_in'][j]).astype(F32)
    z = proj[..., :D_INNER_C]
    xbc = proj[..., D_INNER_C:D_INNER_C + CONV_DIM_C]
    dt = proj[..., D_INNER_C + CONV_DIM_C:]
    xbc, cbuf = _causal_conv(xbc, cbuf0.astype(F32), W['ssm_conv_w'][j], W['ssm_conv_b'][j])
    xbc = jax.nn.silu(xbc)
    xs = xbc[..., :D_INNER_C].reshape(B, L, H_C, P_C)
    Bm = xbc[..., D_INNER_C:D_INNER_C + G_C * N_C].reshape(B, L, G_C, N_C)
    Cm = xbc[..., D_INNER_C + G_C * N_C:].reshape(B, L, G_C, N_C)
    dt = jax.nn.softplus(dt + W['ssm_dt_bias'][j].astype(F32))
    A = -jnp.exp(W['ssm_a_log'][j].astype(F32))
    y, S = _ssd_scan(xs, dt, dt * A, Bm, Cm, S0.astype(F32))
    y = y + W['ssm_d'][j].astype(F32)[:, None] * xs
    y = y.reshape(B, L, D_INNER_C) * jax.nn.silu(z)
    yg = y.reshape(B, L, G_C, D_INNER_C // G_C)
    yg = yg * lax.rsqrt(jnp.mean(yg * yg, axis=-1, keepdims=True) + EPS)
    y = yg.reshape(B, L, D_INNER_C) * W['ssm_gnorm'][j].astype(F32)
    out = y.astype(xn.dtype) @ W['ssm_out'][j]
    return out, S, cbuf


def _swiglu(x, w1, w3, w2):
    return (jax.nn.silu(x @ w1) * (x @ w3)) @ w2


def _ple(x, p_i, w_up, w_gate, g):
    gate = jax.nn.sigmoid((x @ w_gate).astype(F32))
    e = (p_i @ w_up).astype(F32)
    return _rmsnorm(gate * e, g).astype(x.dtype)


def _trunk(x, p, st_hgrn, st_lru_h, st_lru_conv, st_ssm, st_ssm_conv, fresh, W):
    lb_all = jnp.cumsum(jax.nn.softmax(W['hgrn_lb'].astype(F32), axis=0), axis=0)
    lb_all = lb_all - lb_all[0]
    hg, lh, lc, ss, sc = [], [], [], [], []
    for i in range(DEPTH):
        j = i // 2
        xn = _rmsnorm(x, W['g_mix'][i])
        if i % 2 == 0:
            mix, s_a, s_h, s_c = _even_mixer(xn, st_hgrn[j], st_lru_h[j], st_lru_conv[j], fresh, lb_all[j], W, j)
            hg.append(s_a)
            lh.append(s_h)
            lc.append(s_c)
        else:
            mix, s_s, s_c = _odd_mixer(xn, st_ssm[j], st_ssm_conv[j], W, j)
            ss.append(s_s)
            sc.append(s_c)
        x = x + mix
        x = x + _swiglu(_rmsnorm(x, W['g_ffn'][i]), W['ffn_w1'][i], W['ffn_w3'][i], W['ffn_w2'][i])
        x = x + _ple(x, p[i], W['ple_up'][i], W['ple_gate'][i], W['g_ple'][i])
    y = _rmsnorm(x, W['g_final'])
    return (y, jnp.stack(hg).astype(st_hgrn.dtype), jnp.stack(lh).astype(st_lru_h.dtype),
            jnp.stack(lc).astype(st_lru_conv.dtype), jnp.stack(ss).astype(st_ssm.dtype),
            jnp.stack(sc).astype(st_ssm_conv.dtype))


def setup_inputs(seed: int = 0) -> dict:
    key = jax.random.key(seed)
    ks = iter(jax.random.split(key, 64))
    D = D_MODEL

    def nrm(shape, scale):
        return jax.random.normal(next(ks), shape, F32) * scale

    def unif(shape, lo, hi):
        return jax.random.uniform(next(ks), shape, F32, lo, hi)

    a0 = unif((N_EVEN, W_B), 0.9, 0.999) ** (1.0 / LRU_C)
    dt0 = jnp.exp(unif((N_ODD, H_C), math.log(1e-3), math.log(1e-1)))
    inputs = {}
    inputs['x_prompt'] = nrm((BATCH, SEQ, D), 1.0)
    inputs['x_sample'] = nrm((DEC_BATCH, DEC_SEQ, D), 1.0)
    inputs['state_hgrn'] = nrm((N_EVEN, DEC_BATCH, H_A, DK_A, DV_A), 0.5)
    inputs['state_lru_h'] = nrm((N_EVEN, DEC_BATCH, W_B), 0.5)
    inputs['state_lru_conv'] = nrm((N_EVEN, DEC_BATCH, CONV_W - 1, W_B), 1.0)
    inputs['state_ssm'] = nrm((N_ODD, DEC_BATCH, H_C, P_C, N_C), 0.1)
    inputs['state_ssm_conv'] = nrm((N_ODD, DEC_BATCH, CONV_W - 1, CONV_DIM_C), 1.0)
    inputs['p_prompt'] = nrm((DEPTH, BATCH, SEQ, PLE_DIM), 1.0)
    inputs['p_sample'] = nrm((DEPTH, DEC_BATCH, DEC_SEQ, PLE_DIM), 1.0)
    inputs['g_mix'] = 1.0 + nrm((DEPTH, D), 0.05)
    inputs['g_ffn'] = 1.0 + nrm((DEPTH, D), 0.05)
    inputs['g_ple'] = 1.0 + nrm((DEPTH, D), 0.05)
    inputs['g_final'] = 1.0 + nrm((D,), 0.05)
    inputs['w_even_in'] = nrm((N_EVEN, D, IN_EVEN), D ** -0.5)
    inputs['hgrn_lb'] = nrm((N_EVEN, H_A * DK_A), 0.1)
    inputs['hgrn_gnorm'] = 1.0 + nrm((N_EVEN, W_A), 0.05)
    inputs['lru_conv_w'] = nrm((N_EVEN, CONV_W, W_B), CONV_W ** -0.5)
    inputs['lru_conv_b'] = nrm((N_EVEN, W_B), 0.02)
    inputs['lru_wa'] = nrm((N_EVEN, H_B, BLK_B, BLK_B), BLK_B ** -0.5)
    inputs['lru_ba'] = nrm((N_EVEN, H_B, BLK_B), 0.02)
    inputs['lru_wx'] = nrm((N_EVEN, H_B, BLK_B, BLK_B), BLK_B ** -0.5)
    inputs['lru_bx'] = nrm((N_EVEN, H_B, BLK_B), 0.02)
    inputs['lru_lam'] = jnp.log(a0) - jnp.log1p(-a0)
    inputs['w_even_out'] = nrm((N_EVEN, W_A + W_B, D), (W_A + W_B) ** -0.5)
    inputs['ssm_in'] = nrm((N_ODD, D, IN_C), D ** -0.5)
    inputs['ssm_conv_w'] = nrm((N_ODD, CONV_W, CONV_DIM_C), CONV_W ** -0.5)
    inputs['ssm_conv_b'] = nrm((N_ODD, CONV_DIM_C), 0.02)
    inputs['ssm_dt_bias'] = dt0 + jnp.log(-jnp.expm1(-dt0))
    inputs['ssm_a_log'] = jnp.log(unif((N_ODD, H_C), 1.0, 16.0))
    inputs['ssm_d'] = 1.0 + nrm((N_ODD, H_C), 0.1)
    inputs['ssm_gnorm'] = 1.0 + nrm((N_ODD, D_INNER_C), 0.05)
    inputs['ssm_out'] = nrm((N_ODD, D_INNER_C, D), D_INNER_C ** -0.5)
    inputs['ffn_w1'] = nrm((DEPTH, D, D_FF), D ** -0.5)
    inputs['ffn_w3'] = nrm((DEPTH, D, D_FF), D ** -0.5)
    inputs['ffn_w2'] = nrm((DEPTH, D_FF, D), D_FF ** -0.5)
    inputs['ple_up'] = nrm((DEPTH, PLE_DIM, D), PLE_DIM ** -0.5)
    inputs['ple_gate'] = nrm((DEPTH, D, D), D ** -0.5)
    return inputs


def reference(x_prompt, x_sample, state_hgrn, state_lru_h, state_lru_conv, state_ssm, state_ssm_conv,
              p_prompt, p_sample, g_mix, g_ffn, g_ple, g_final, w_even_in, hgrn_lb, hgrn_gnorm,
              lru_conv_w, lru_conv_b, lru_wa, lru_ba, lru_wx, lru_bx, lru_lam, w_even_out,
              ssm_in, ssm_conv_w, ssm_conv_b, ssm_dt_bias, ssm_a_log, ssm_d, ssm_gnorm, ssm_out,
              ffn_w1, ffn_w3, ffn_w2, ple_up, ple_gate):
    W = dict(g_mix=g_mix, g_ffn=g_ffn, g_ple=g_ple, g_final=g_final, w_even_in=w_even_in,
             hgrn_lb=hgrn_lb, hgrn_gnorm=hgrn_gnorm, lru_conv_w=lru_conv_w, lru_conv_b=lru_conv_b,
             lru_wa=lru_wa, lru_ba=lru_ba, lru_wx=lru_wx, lru_bx=lru_bx, lru_lam=lru_lam,
             w_even_out=w_even_out, ssm_in=ssm_in, ssm_conv_w=ssm_conv_w, ssm_conv_b=ssm_conv_b,
             ssm_dt_bias=ssm_dt_bias, ssm_a_log=ssm_a_log, ssm_d=ssm_d, ssm_gnorm=ssm_gnorm,
             ssm_out=ssm_out, ffn_w1=ffn_w1, ffn_w3=ffn_w3, ffn_w2=ffn_w2, ple_up=ple_up,
             ple_gate=ple_gate)
    bp = x_prompt.shape[0]
    dtp = x_prompt.dtype
    z_hgrn = jnp.zeros((N_EVEN, bp, H_A, DK_A, DV_A), dtp)
    z_lru_h = jnp.zeros((N_EVEN, bp, W_B), dtp)
    z_lru_conv = jnp.zeros((N_EVEN, bp, CONV_W - 1, W_B), dtp)
    z_ssm = jnp.zeros((N_ODD, bp, H_C, P_C, N_C), dtp)
    z_ssm_conv = jnp.zeros((N_ODD, bp, CONV_W - 1, CONV_DIM_C), dtp)
    y_prompt, hg_p, lh_p, lc_p, ss_p, sc_p = _trunk(
        x_prompt, p_prompt, z_hgrn, z_lru_h, z_lru_conv, z_ssm, z_ssm_conv, True, W)
    y_sample, hg_s, lh_s, lc_s, ss_s, sc_s = _trunk(
        x_sample, p_sample, state_hgrn, state_lru_h, state_lru_conv, state_ssm, state_ssm_conv, False, W)
    return (y_prompt, y_sample, hg_p, hg_s, lh_p, lh_s, lc_p, lc_s, ss_p, ss_s, sc_p, sc_s)
```

```python
import functools
import math

import jax
import jax.numpy as jnp
from jax import lax
from jax.experimental import pallas as pl
from jax.experimental.pallas import tpu as pltpu

F32 = jnp.float32
BF16 = jnp.bfloat16
EPS = 1e-6
LRU_C = 8.0
CONV_W = 4
HIST = CONV_W - 1
SUBLANES = 8
VMEM_LIMIT = 56 * 1024 * 1024


def _dot(a, b):
    return jnp.dot(a, b, preferred_element_type=F32)


def _dot_tn(a, b):
    return lax.dot_general(a, b, (((0,), (0,)), ((), ())), preferred_element_type=F32)


def _dot_nt(a, b):
    return lax.dot_general(a, b, (((1,), (1,)), ((), ())), preferred_element_type=F32)


def _dot_exact(m, x):
    hi = x.astype(BF16)
    r1 = x - hi.astype(F32)
    mid = r1.astype(BF16)
    lo = (r1 - mid.astype(F32)).astype(BF16)
    return _dot(m, hi) + _dot(m, mid) + _dot(m, lo)


def _rms(x, g):
    return x * lax.rsqrt(jnp.mean(x * x, axis=-1, keepdims=True) + EPS) * g


def _sigmoid(x):
    return jax.nn.sigmoid(x)


def _silu(x):
    return x * jax.nn.sigmoid(x)


def _softplus(x):
    return jnp.maximum(x, 0.0) + jnp.log1p(jnp.exp(-jnp.abs(x)))


def _log_sigmoid(x):
    return jnp.minimum(x, 0.0) - jnp.log1p(jnp.exp(-jnp.abs(x)))


def _logaddexp(a, b):
    return jnp.maximum(a, b) + jnp.log1p(jnp.exp(-jnp.abs(a - b)))


def _gelu_tanh(x):
    return 0.5 * x * (1.0 + jnp.tanh(math.sqrt(2.0 / math.pi) * (x + 0.044715 * (x * x * x))))


def _seq_tri(rows, t_len):
    shift = t_len.bit_length() - 1
    r = lax.broadcasted_iota(jnp.int32, (rows, rows), 0)
    c = lax.broadcasted_iota(jnp.int32, (rows, rows), 1)
    return ((r >> shift) == (c >> shift)) & (c <= r)


def _seq_last(x, nseq, t_len):
    x3 = x.reshape(nseq, t_len, x.shape[-1])
    last = x3[:, t_len - 1:t_len, :]
    return jnp.broadcast_to(last, x3.shape).reshape(x.shape)


def _conv_step(raw, ext_ref, w, bias, nseq, t_len):
    rows, ch = raw.shape
    ext_ref[:, SUBLANES:SUBLANES + t_len, :] = raw.reshape(nseq, t_len, ch)
    out = raw * w[HIST:HIST + 1, :] + bias
    for j in range(1, CONV_W):
        shifted = ext_ref[:, SUBLANES - j:SUBLANES - j + t_len, :].reshape(rows, ch)
        out = out + shifted * w[HIST - j:HIST - j + 1, :]
    new_hist = ext_ref[:, SUBLANES + t_len - HIST:SUBLANES + t_len, :]
    ext_ref[:, SUBLANES - HIST:SUBLANES, :] = new_hist
    return out, new_hist


def _proj_kernel(x_ref, g_ref, w_ref, o_ref):
    xn = _rms(x_ref[...], g_ref[...]).astype(BF16)
    o_ref[...] = _dot(xn, w_ref[...])


def _resident(shape):
    nd = len(shape)
    return pl.BlockSpec(shape, lambda *_: (0,) * nd, pipeline_mode=pl.Buffered(1))


def _norm_proj(x, g, w, tm, name):
    m, d = x.shape
    n = w.shape[1]
    return pl.pallas_call(
        _proj_kernel,
        grid=(m // tm,),
        in_specs=[pl.BlockSpec((tm, d), lambda i: (i, 0)), _resident((1, d)), _resident((d, n))],
        out_specs=pl.BlockSpec((tm, n), lambda i: (i, 0)),
        out_shape=jax.ShapeDtypeStruct((m, n), F32),
        compiler_params=pltpu.CompilerParams(dimension_semantics=("parallel",),
                                             vmem_limit_bytes=VMEM_LIMIT),
        name=name,
    )(x, g.reshape(1, d), w)


def _post_kernel(x_ref, act_ref, p_ref, wo_ref, gf_ref, w1_ref, w3_ref, w2_ref, wg_ref, wu_ref,
                 gp_ref, gfin_ref, o_ref, *, final):
    x = x_ref[...] + _dot(act_ref[...].astype(BF16), wo_ref[...])
    xn = _rms(x, gf_ref[...]).astype(BF16)
    h = (_silu(_dot(xn, w1_ref[...])) * _dot(xn, w3_ref[...])).astype(BF16)
    x = x + _dot(h, w2_ref[...])
    gate = _sigmoid(_dot(x.astype(BF16), wg_ref[...]))
    emb = _dot(p_ref[...].astype(BF16), wu_ref[...])
    x = x + _rms(gate * emb, gp_ref[...])
    if final:
        x = _rms(x, gfin_ref[...])
    o_ref[...] = x


def _post_mixer(x, act, p, wo, gf, w1, w3, w2, wg, wu, gp, gfin, final, tm, name):
    m, d = x.shape
    ka = act.shape[1]
    dp = p.shape[1]
    dff = w1.shape[1]
    row = lambda i: (i, 0)
    return pl.pallas_call(
        functools.partial(_post_kernel, final=final),
        grid=(m // tm,),
        in_specs=[pl.BlockSpec((tm, d), row), pl.BlockSpec((tm, ka), row), pl.BlockSpec((tm, dp), row),
                  _resident((ka, d)), _resident((1, d)), _resident((d, dff)), _resident((d, dff)),
                  _resident((dff, d)), _resident((d, d)), _resident((dp, d)), _resident((1, d)),
                  _resident((1, d))],
        out_specs=pl.BlockSpec((tm, d), row),
        out_shape=jax.ShapeDtypeStruct((m, d), F32),
        compiler_params=pltpu.CompilerParams(dimension_semantics=("parallel",),
                                             vmem_limit_bytes=VMEM_LIMIT),
        name=name,
    )(x, act, p, wo, gf.reshape(1, d), w1, w3, w2, wg, wu, gp.reshape(1, d), gfin.reshape(1, d))


def _even_kernel(proj_ref, s0_ref, h0_ref, c0_ref, lbp_ref, gn_ref, cw_ref, cb_ref, wa_ref, ba_ref,
                 wx_ref, bx_ref, lam_ref, act_ref, s_ref, h_ref, c_ref, ext_ref,
                 *, nseq, t_len, n_heads, fresh):
    rows = nseq * t_len
    wid = proj_ref.shape[1] // 6
    dk = wid // n_heads
    levels = t_len.bit_length() - 1
    chunk = pl.program_id(1)

    @pl.when(chunk == 0)
    def _():
        s_ref[...] = s0_ref[...]
        h_ref[...] = h0_ref[...]
        ext_ref[:, SUBLANES - HIST:SUBLANES, :] = c0_ref[...]

    t_in = lax.broadcasted_iota(jnp.int32, (rows, wid), 0) & (t_len - 1)

    u, new_hist = _conv_step(proj_ref[:, 5 * wid:6 * wid], ext_ref, cw_ref[...], cb_ref[...], nseq, t_len)
    c_ref[...] = new_hist
    u16 = u.astype(BF16)
    hs = [slice(h * dk, (h + 1) * dk) for h in range(n_heads)]
    r = _sigmoid(jnp.concatenate([_dot(u16[:, s], wa_ref[h]) for h, s in enumerate(hs)], axis=1) + ba_ref[...])
    gi = _sigmoid(jnp.concatenate([_dot(u16[:, s], wx_ref[h]) for h, s in enumerate(hs)], axis=1) + bx_ref[...])
    log_a = (-LRU_C) * r * _softplus(-lam_ref[...])
    a = jnp.exp(log_a)
    mult = jnp.sqrt(-jnp.tanh(log_a) * (jnp.exp(2.0 * log_a) + 1.0))
    if fresh:
        mult = jnp.where((t_in == 0) & (chunk == 0), 1.0, mult)
    bt = mult * (gi * u)
    shift = 1
    while shift < t_len:
        valid = t_in >= shift
        a_sh = pltpu.roll(a, shift, 0)
        b_sh = pltpu.roll(bt, shift, 0)
        bt = jnp.where(valid, a * b_sh + bt, bt)
        a = jnp.where(valid, a * a_sh, a)
        shift *= 2
    h_prev = jnp.broadcast_to(h_ref[...], (nseq, t_len, wid)).reshape(rows, wid)
    hseq = a * h_prev + bt
    h_ref[...] = hseq.reshape(nseq, t_len, wid)[:, t_len - 1:t_len, :]
    act_ref[:, wid:2 * wid] = (_gelu_tanh(proj_ref[:, 4 * wid:5 * wid]) * hseq).astype(act_ref.dtype)

    lbp = lbp_ref[...]
    fz = proj_ref[:, wid:2 * wid]
    logf = _logaddexp(lbp[0:1, :], lbp[1:2, :] + _log_sigmoid(fz))
    k = lbp[2:3, :] * _sigmoid(-fz)
    q = _silu(proj_ref[:, 0:wid])
    v16 = proj_ref[:, 2 * wid:3 * wid].astype(BF16)
    tri = _seq_tri(rows, t_len)
    b = _dot_exact(tri.astype(BF16), logf)
    b_last = _seq_last(b, nseq, t_len)
    qe16 = (q * jnp.exp(b)).astype(BF16)
    kd16 = (k * jnp.exp(b_last - b)).astype(BF16)

    ql, kl = [q.astype(BF16)], [k.astype(BF16)]
    for lvl in range(1, levels + 1):
        blk = 1 << lvl
        if lvl == 1:
            e = jnp.where((t_in & 1) == 1, logf, 0.0)
        elif lvl == 2:
            nxt = pltpu.roll(logf, rows - 1, 0)
            prv = pltpu.roll(logf, 1, 0)
            pos = t_in & 3
            e = jnp.where(pos == 0, nxt, jnp.where(pos == 1, 0.0, jnp.where(pos == 2, logf, logf + prv)))
        else:
            b3 = b.reshape(rows // blk, blk, wid)
            mid = jnp.broadcast_to(b3[:, blk // 2 - 1:blk // 2, :], b3.shape).reshape(rows, wid)
            e = -jnp.abs(b - mid)
        w = jnp.exp(e)
        ql.append((q * w).astype(BF16))
        kl.append((k * w).astype(BF16))

    r_i = lax.broadcasted_iota(jnp.int32, (rows, rows), 0)
    c_i = lax.broadcasted_iota(jnp.int32, (rows, rows), 1)
    diff = r_i ^ c_i
    below = c_i < r_i
    masks = [r_i == c_i] + [below & ((diff >> (lvl - 1)) == 1) for lvl in range(1, levels + 1)]

    gn = gn_ref[...]
    g_act = _silu(proj_ref[:, 3 * wid:4 * wid])
    for h, s in enumerate(hs):
        att = jnp.zeros((rows, rows), F32)
        for lvl in range(levels + 1):
            att = att + jnp.where(masks[lvl], _dot_nt(ql[lvl][:, s], kl[lvl][:, s]), 0.0)
        o = _dot(att.astype(BF16), v16[:, s])
        inter = []
        for i in range(nseq):
            rs = slice(i * t_len, (i + 1) * t_len)
            s_old = s_ref[i, h]
            inter.append(_dot(qe16[rs, s], s_old.astype(BF16)))
            e_last = jnp.exp(b[(i + 1) * t_len - 1:(i + 1) * t_len, s])
            scale = jnp.transpose(jnp.broadcast_to(e_last, (dk, dk)))
            s_ref[i, h] = scale * s_old + _dot_tn(kd16[rs, s], v16[rs, s])
        o = o + (inter[0] if nseq == 1 else jnp.concatenate(inter, axis=0))
        o = o * lax.rsqrt(jnp.mean(o * o, axis=-1, keepdims=True) + EPS)
        act_ref[:, s] = (o * gn[:, s] * g_act[:, s]).astype(act_ref.dtype)


def _even_mixer(proj, s0, h0, c0, lbp, gn, cw, cb, wa, ba, wx, bx, lam, *, n_batch, seq_len, nseq, t_len,
                fresh, act_dtype, name):
    n_heads, dk = s0.shape[1], s0.shape[2]
    wid = n_heads * dk
    rows = nseq * t_len
    n_chunks = seq_len // t_len
    blk = lambda b, c: (b, 0, 0)
    blk4 = lambda b, c: (b, 0, 0, 0)
    kern = functools.partial(_even_kernel, nseq=nseq, t_len=t_len, n_heads=n_heads, fresh=fresh)
    return pl.pallas_call(
        kern,
        grid=(n_batch // nseq, n_chunks),
        in_specs=[pl.BlockSpec((rows, 6 * wid), lambda b, c: (b * n_chunks + c, 0)),
                  pl.BlockSpec((nseq, n_heads, dk, dk), blk4),
                  pl.BlockSpec((nseq, 1, wid), blk),
                  pl.BlockSpec((nseq, HIST, wid), blk),
                  _resident((3, wid)), _resident((1, wid)), _resident((CONV_W, wid)), _resident((1, wid)),
                  _resident(wa.shape), _resident((1, wid)), _resident(wx.shape), _resident((1, wid)),
                  _resident((1, wid))],
        out_specs=[pl.BlockSpec((rows, 2 * wid), lambda b, c: (b * n_chunks + c, 0)),
                   pl.BlockSpec((nseq, n_heads, dk, dk), blk4),
                   pl.BlockSpec((nseq, 1, wid), blk),
                   pl.BlockSpec((nseq, HIST, wid), blk)],
        out_shape=[jax.ShapeDtypeStruct((n_batch * seq_len, 2 * wid), act_dtype),
                   jax.ShapeDtypeStruct(s0.shape, F32),
                   jax.ShapeDtypeStruct((n_batch, 1, wid), F32),
                   jax.ShapeDtypeStruct((n_batch, HIST, wid), F32)],
        scratch_shapes=[pltpu.VMEM((nseq, t_len + SUBLANES, wid), F32)],
        compiler_params=pltpu.CompilerParams(dimension_semantics=("parallel", "arbitrary"),
                                             vmem_limit_bytes=VMEM_LIMIT),
        name=name,
    )(proj, s0, h0.reshape(n_batch, 1, wid), c0, lbp, gn.reshape(1, wid), cw, cb.reshape(1, wid),
      wa, ba.reshape(1, wid), wx, bx.reshape(1, wid), lam.reshape(1, wid))


def _odd_kernel(proj_ref, s0_ref, c0_ref, cw_ref, cb_ref, dtb_ref, alog_ref, dx_ref, gn_ref,
                act_ref, s_ref, c_ref, ext_ref, y_ref, *, nseq, t_len, n_heads, n_groups, n_state):
    rows = nseq * t_len
    d_inner = act_ref.shape[1]
    p_dim = d_inner // n_heads
    gw = n_groups * n_state
    hpg = n_heads // n_groups
    gcols = hpg * p_dim
    lanes = 2 * p_dim
    chunk = pl.program_id(1)

    @pl.when(chunk == 0)
    def _():
        s_ref[...] = s0_ref[...]
        ext_ref[:, SUBLANES - HIST:SUBLANES, :] = c0_ref[...]

    conv, new_hist = _conv_step(proj_ref[:, d_inner:2 * d_inner + 2 * gw], ext_ref, cw_ref[...], cb_ref[...],
                                nseq, t_len)
    c_ref[...] = new_hist
    xbc = _silu(conv)
    xs = xbc[:, 0:d_inner]
    b16 = xbc[:, d_inner:d_inner + gw].astype(BF16)
    c16 = xbc[:, d_inner + gw:d_inner + 2 * gw].astype(BF16)

    dt = _softplus(proj_ref[:, 2 * d_inner + 2 * gw:] + dtb_ref[...])
    la = dt * (-jnp.exp(alog_ref[...]))
    tri = _seq_tri(rows, t_len)
    cum = _dot_exact(tri.astype(BF16), la)
    cum_last = _seq_last(cum, nseq, t_len)
    e_cum = jnp.exp(cum)
    w_upd = jnp.exp(cum_last - cum) * dt
    cum_t = jnp.transpose(cum)
    dt_t = jnp.transpose(dt)

    lane = lax.broadcasted_iota(jnp.int32, (rows, lanes), 1)
    low = lane < p_dim

    def pair_cols(x, h):
        return jnp.where(low, jnp.broadcast_to(x[:, h:h + 1], (rows, lanes)),
                         jnp.broadcast_to(x[:, h + 1:h + 2], (rows, lanes)))

    for g in range(n_groups):
        gs = slice(g * n_state, (g + 1) * n_state)
        cb_g = _dot_nt(c16[:, gs], b16[:, gs])
        y_int, x_upd = [], []
        for i in range(nseq):
            rs = slice(i * t_len, (i + 1) * t_len)
            s_old = s_ref[i, g * gcols:(g + 1) * gcols, :]
            y_int.append(_dot_nt(c16[rs, gs], s_old.astype(BF16)))
        y_int = y_int[0] if nseq == 1 else jnp.concatenate(y_int, axis=0)
        for hp in range(hpg // 2):
            h0 = g * hpg + 2 * hp
            cols = slice(h0 * p_dim, (h0 + 2) * p_dim)
            x_pair = xs[:, cols]
            y_pair = jnp.zeros((rows, lanes), F32)
            for sub in range(2):
                h = h0 + sub
                seg = jnp.broadcast_to(cum[:, h:h + 1], (rows, rows)) - jnp.broadcast_to(cum_t[h:h + 1, :], (rows, rows))
                decay = jnp.exp(jnp.where(tri, seg, -jnp.inf))
                m = (cb_g * decay * jnp.broadcast_to(dt_t[h:h + 1, :], (rows, rows))).astype(BF16)
                x_half = jnp.where(low if sub == 0 else ~low, x_pair, 0.0).astype(BF16)
                y_pair = y_pair + _dot(m, x_half)
            y_pair = y_pair + pair_cols(e_cum, h0) * y_int[:, 2 * hp * p_dim:(2 * hp + 2) * p_dim]
            y_pair = y_pair + dx_ref[:, cols] * x_pair
            y_ref[:, cols] = y_pair
            x_upd.append((x_pair * pair_cols(w_upd, h0)).astype(BF16))
        x_upd = jnp.concatenate(x_upd, axis=1)
        for i in range(nseq):
            rs = slice(i * t_len, (i + 1) * t_len)
            upd = _dot_tn(x_upd[rs, :], b16[rs, gs])
            last_t = jnp.transpose(jnp.broadcast_to(cum[(i + 1) * t_len - 1:(i + 1) * t_len, :],
                                                    (cum.shape[1], cum.shape[1])))
            for hl in range(hpg):
                h = g * hpg + hl
                hr = slice(h * p_dim, (h + 1) * p_dim)
                dec = jnp.exp(jnp.broadcast_to(last_t[h:h + 1, :], (p_dim, n_state)))
                s_ref[i, hr, :] = dec * s_ref[i, hr, :] + upd[hl * p_dim:(hl + 1) * p_dim, :]

    z = proj_ref[:, 0:d_inner]
    gn = gn_ref[...]
    norm_w = d_inner // n_groups
    for g in range(n_groups):
        cols = slice(g * norm_w, (g + 1) * norm_w)
        y = y_ref[:, cols] * _silu(z[:, cols])
        y = y * lax.rsqrt(jnp.mean(y * y, axis=-1, keepdims=True) + EPS)
        act_ref[:, cols] = (y * gn[:, cols]).astype(act_ref.dtype)


def _odd_mixer(proj, s0, c0, cw, cb, dtb, alog, dx, gn, *, n_batch, seq_len, nseq, t_len, n_heads, n_groups,
               act_dtype, name):
    hp_rows, n_state = s0.shape[1], s0.shape[2]
    d_inner = hp_rows
    conv_dim = c0.shape[2]
    rows = nseq * t_len
    n_chunks = seq_len // t_len
    blk = lambda b, c: (b, 0, 0)
    kern = functools.partial(_odd_kernel, nseq=nseq, t_len=t_len, n_heads=n_heads, n_groups=n_groups,
                             n_state=n_state)
    state_spec = pl.BlockSpec((nseq, hp_rows, n_state), blk,
                              pipeline_mode=pl.Buffered(1) if nseq > 4 else None)
    return pl.pallas_call(
        kern,
        grid=(n_batch // nseq, n_chunks),
        in_specs=[pl.BlockSpec((rows, proj.shape[1]), lambda b, c: (b * n_chunks + c, 0)),
                  state_spec,
                  pl.BlockSpec((nseq, HIST, conv_dim), blk),
                  _resident((CONV_W, conv_dim)), _resident((1, conv_dim)), _resident(dtb.shape),
                  _resident(alog.shape), _resident((1, d_inner)), _resident((1, d_inner))],
        out_specs=[pl.BlockSpec((rows, d_inner), lambda b, c: (b * n_chunks + c, 0)),
                   state_spec,
                   pl.BlockSpec((nseq, HIST, conv_dim), blk)],
        out_shape=[jax.ShapeDtypeStruct((n_batch * seq_len, d_inner), act_dtype),
                   jax.ShapeDtypeStruct(s0.shape, F32),
                   jax.ShapeDtypeStruct(c0.shape, F32)],
        scratch_shapes=[pltpu.VMEM((nseq, t_len + SUBLANES, conv_dim), F32),
                        pltpu.VMEM((rows, d_inner), F32)],
        compiler_params=pltpu.CompilerParams(dimension_semantics=("parallel", "arbitrary"),
                                             vmem_limit_bytes=VMEM_LIMIT),
        name=name,
    )(proj, s0, c0, cw, cb.reshape(1, conv_dim), dtb, alog, dx, gn.reshape(1, d_inner))


def _trunk(x, p, st_hgrn, st_lru_h, st_lru_conv, st_ssm, st_ssm_conv, fresh, W, cfg, tag):
    n_batch, seq_len, d = x.shape
    m = n_batch * seq_len
    depth = p.shape[0]
    nseq, t_len, tm_proj, tm_post, act_dtype = cfg
    n_heads_c, p_c, n_c = st_ssm.shape[2], st_ssm.shape[3], st_ssm.shape[4]
    d_inner = n_heads_c * p_c
    x = x.reshape(m, d)
    hg, lh, lc, ss, sc = [], [], [], [], []
    for i in range(depth):
        j = i // 2
        final = i == depth - 1
        if i % 2 == 0:
            proj = _norm_proj(x, W['g_mix'][i], W['w_even_in'][j], tm_proj, f"{tag}_proj{i}")
            act, s_a, s_h, s_c = _even_mixer(
                proj, st_hgrn[j], st_lru_h[j], st_lru_conv[j], W['lbp'][j], W['hgrn_gnorm'][j],
                W['lru_conv_w'][j], W['lru_conv_b'][j], W['lru_wa'][j], W['lru_ba'][j].reshape(-1),
                W['lru_wx'][j], W['lru_bx'][j].reshape(-1), W['lru_lam'][j],
                n_batch=n_batch, seq_len=seq_len, nseq=nseq, t_len=t_len, fresh=fresh,
                act_dtype=act_dtype, name=f"{tag}_even{i}")
            hg.append(s_a)
            lh.append(s_h.reshape(n_batch, -1))
            lc.append(s_c)
            wo = W['w_even_out'][j]
        else:
            proj = _norm_proj(x, W['g_mix'][i], W['ssm_in'][j], tm_proj, f"{tag}_proj{i}")
            act, s_s, s_c = _odd_mixer(
                proj, st_ssm[j].reshape(n_batch, d_inner, n_c), st_ssm_conv[j], W['ssm_conv_w'][j],
                W['ssm_conv_b'][j], W['ssm_dt_bias'][j], W['ssm_a_log'][j], W['ssm_dx'][j], W['ssm_gnorm'][j],
                n_batch=n_batch, seq_len=seq_len, nseq=nseq, t_len=t_len, n_heads=n_heads_c,
                n_groups=W['n_groups'], act_dtype=act_dtype, name=f"{tag}_odd{i}")
            ss.append(s_s.reshape(n_batch, n_heads_c, p_c, n_c))
            sc.append(s_c)
            wo = W['ssm_out'][j]
        x = _post_mixer(x, act, p[i].reshape(m, -1), wo, W['g_ffn'][i], W['ffn_w1'][i], W['ffn_w3'][i],
                        W['ffn_w2'][i], W['ple_gate'][i], W['ple_up'][i], W['g_ple'][i], W['g_final'],
                        final, tm_post, f"{tag}_post{i}")
    return (x.reshape(n_batch, seq_len, d), jnp.stack(hg), jnp.stack(lh), jnp.stack(lc), jnp.stack(ss),
            jnp.stack(sc))


def kernel(x_prompt, x_sample, state_hgrn, state_lru_h, state_lru_conv, state_ssm, state_ssm_conv, p_prompt, p_sample, g_mix, g_ffn, g_ple, g_final, w_even_in, hgrn_lb, hgrn_gnorm, lru_conv_w, lru_conv_b, lru_wa, lru_ba, lru_wx, lru_bx, lru_lam, w_even_out, ssm_in, ssm_conv_w, ssm_conv_b, ssm_dt_bias, ssm_a_log, ssm_d, ssm_gnorm, ssm_out, ffn_w1, ffn_w3, ffn_w2, ple_up, ple_gate):
    n_even = state_hgrn.shape[0]
    n_odd, _, n_heads_c, p_c, n_c = state_ssm.shape
    d_inner = n_heads_c * p_c
    conv_dim = state_ssm_conv.shape[-1]
    n_groups = (conv_dim - d_inner) // (2 * n_c)
    lane_blk = 128

    lb = jnp.cumsum(jax.nn.softmax(hgrn_lb.astype(F32), axis=0), axis=0)
    lb = lb - lb[0]
    lbp = jnp.stack([jnp.log(lb), jnp.log1p(-lb), 1.0 - lb], axis=1)
    in_c = ssm_in.shape[-1]
    in_pad = -(-in_c // lane_blk) * lane_blk
    head_pad = in_pad - (in_c - n_heads_c)
    ssm_in_p = jnp.pad(ssm_in, ((0, 0), (0, 0), (0, in_pad - in_c)))
    pad_h = lambda a: jnp.pad(a.astype(F32), ((0, 0), (0, head_pad - n_heads_c))).reshape(n_odd, 1, head_pad)
    W = dict(
        g_mix=g_mix, g_ffn=g_ffn, g_ple=g_ple, g_final=g_final, lbp=lbp, hgrn_gnorm=hgrn_gnorm,
        lru_conv_w=lru_conv_w, lru_conv_b=lru_conv_b, lru_ba=lru_ba, lru_bx=lru_bx, lru_lam=lru_lam,
        ssm_conv_w=ssm_conv_w, ssm_conv_b=ssm_conv_b, ssm_gnorm=ssm_gnorm,
        ssm_dt_bias=pad_h(ssm_dt_bias), ssm_a_log=pad_h(ssm_a_log),
        ssm_dx=jnp.repeat(ssm_d.astype(F32), p_c, axis=1).reshape(n_odd, 1, d_inner),
        w_even_in=w_even_in.astype(BF16), lru_wa=lru_wa.astype(BF16), lru_wx=lru_wx.astype(BF16),
        w_even_out=w_even_out.astype(BF16), ssm_in=ssm_in_p.astype(BF16), ssm_out=ssm_out.astype(BF16),
        ffn_w1=ffn_w1.astype(BF16), ffn_w3=ffn_w3.astype(BF16), ffn_w2=ffn_w2.astype(BF16),
        ple_up=ple_up.astype(BF16), ple_gate=ple_gate.astype(BF16), n_groups=n_groups)

    bp = x_prompt.shape[0]
    zeros = lambda ref: jnp.zeros((ref.shape[0], bp) + ref.shape[2:], F32)
    cfg_prompt = (1, 256, 512, 256, BF16)
    cfg_sample = (16, x_sample.shape[1], 512, 256, F32)
    y_p, hg_p, lh_p, lc_p, ss_p, sc_p = _trunk(
        x_prompt, p_prompt, zeros(state_hgrn), zeros(state_lru_h), zeros(state_lru_conv), zeros(state_ssm),
        zeros(state_ssm_conv), True, W, cfg_prompt, "prompt")
    y_s, hg_s, lh_s, lc_s, ss_s, sc_s = _trunk(
        x_sample, p_sample, state_hgrn, state_lru_h, state_lru_conv, state_ssm, state_ssm_conv, False, W,
        cfg_sample, "sample")
    return (y_p, y_s, hg_p, hg_s, lh_p, lh_s, lc_p, lc_s, ss_p, ss_s, sc_p, sc_s)
```

```python
import functools
import math

import jax
import jax.numpy as jnp
from jax import lax
from jax.experimental import pallas as pl
from jax.experimental.pallas import tpu as pltpu

F32 = jnp.float32
BF16 = jnp.bfloat16
EPS = 1e-6
LRU_C = 8.0
CONV_W = 4
HIST = CONV_W - 1
SUBLANES = 8
VMEM_LIMIT = 56 * 1024 * 1024


def _dot(a, b):
    return jnp.dot(a, b, preferred_element_type=F32)


def _dot_tn(a, b):
    return lax.dot_general(a, b, (((0,), (0,)), ((), ())), preferred_element_type=F32)


def _dot_nt(a, b):
    return lax.dot_general(a, b, (((1,), (1,)), ((), ())), preferred_element_type=F32)


def _dot_exact(m, x):
    hi = x.astype(BF16)
    r1 = x - hi.astype(F32)
    mid = r1.astype(BF16)
    lo = (r1 - mid.astype(F32)).astype(BF16)
    return _dot(m, hi) + _dot(m, mid) + _dot(m, lo)


def _rms(x, g):
    return x * lax.rsqrt(jnp.mean(x * x, axis=-1, keepdims=True) + EPS) * g


def _sigmoid(x):
    return jax.nn.sigmoid(x)


def _silu(x):
    return x * jax.nn.sigmoid(x)


def _softplus(x):
    return jnp.maximum(x, 0.0) + jnp.log1p(jnp.exp(-jnp.abs(x)))


def _log_sigmoid(x):
    return jnp.minimum(x, 0.0) - jnp.log(1.0 + jnp.exp(-jnp.abs(x)))


def _logaddexp(a, b):
    return jnp.maximum(a, b) + jnp.log(1.0 + jnp.exp(-jnp.abs(a - b)))


def _neg_expm1(y):
    u = jnp.exp(y)
    safe = (u - 1.0) * y / jnp.log(jnp.where((u == 1.0) | (u == 0.0), 0.5, u))
    return -jnp.where(u == 1.0, y, jnp.where(u == 0.0, -1.0, safe))


def _gelu_tanh(x):
    return 0.5 * x * (1.0 + jnp.tanh(math.sqrt(2.0 / math.pi) * (x + 0.044715 * (x * x * x))))


def _seq_tri(rows, t_len):
    shift = t_len.bit_length() - 1
    r = lax.broadcasted_iota(jnp.int32, (rows, rows), 0)
    c = lax.broadcasted_iota(jnp.int32, (rows, rows), 1)
    return ((r >> shift) == (c >> shift)) & (c <= r)


def _seq_last(x, nseq, t_len):
    x3 = x.reshape(nseq, t_len, x.shape[-1])
    last = x3[:, t_len - 1:t_len, :]
    return jnp.broadcast_to(last, x3.shape).reshape(x.shape)


def _conv_step(raw, hist_ref, w, bias, nseq, t_len):
    rows, ch = raw.shape
    raw3 = raw.reshape(nseq, t_len, ch)
    hist = hist_ref[...].reshape(nseq * SUBLANES, ch)
    row = lax.broadcasted_iota(jnp.int32, (nseq, SUBLANES, ch), 1)
    out = raw * w[HIST:HIST + 1, :] + bias
    for j in range(1, CONV_W):
        rolled = pltpu.roll(raw, j, 0).reshape(nseq, t_len, ch)
        hist_j = pltpu.roll(hist, (nseq * SUBLANES + j - SUBLANES) % (nseq * SUBLANES), 0)
        head = jnp.where(row < j, hist_j.reshape(nseq, SUBLANES, ch), rolled[:, 0:SUBLANES, :])
        shifted = head if t_len == SUBLANES else jnp.concatenate([head, rolled[:, SUBLANES:, :]], axis=1)
        out = out + shifted.reshape(rows, ch) * w[HIST - j:HIST - j + 1, :]
    hist_ref[...] = raw3[:, t_len - SUBLANES:, :]
    return out, raw3[:, t_len - HIST:, :]


def _resident(shape, layer=None):
    nd = len(shape)
    if layer is None:
        return pl.BlockSpec(shape, lambda *_: (0,) * nd, pipeline_mode=pl.Buffered(1))
    return pl.BlockSpec((None,) + tuple(shape), lambda *_: (layer,) + (0,) * nd, pipeline_mode=pl.Buffered(1))


def _compiler_params(semantics):
    return pltpu.CompilerParams(dimension_semantics=semantics, vmem_limit_bytes=VMEM_LIMIT)


def _proj_kernel(x_ref, g_ref, w_ref, o_ref):
    xn = _rms(x_ref[...], g_ref[...]).astype(BF16)
    o_ref[...] = _dot(xn, w_ref[...])


def _norm_proj(x, g, w, layer, w_layer, tm, name):
    m, d = x.shape
    n = w.shape[-1]
    return pl.pallas_call(
        _proj_kernel,
        grid=(m // tm,),
        in_specs=[pl.BlockSpec((tm, d), lambda i: (i, 0)), _resident((1, d), layer), _resident((d, n), w_layer)],
        out_specs=pl.BlockSpec((tm, n), lambda i: (i, 0)),
        out_shape=jax.ShapeDtypeStruct((m, n), F32),
        compiler_params=_compiler_params(("parallel",)),
        name=name,
    )(x, g, w)


def _post_kernel(x_ref, act_ref, p_ref, wo_ref, gf_ref, w1_ref, w3_ref, w2_ref, wg_ref, wu_ref,
                 gp_ref, gfin_ref, o_ref, *, final):
    x = x_ref[...] + _dot(act_ref[...].astype(BF16), wo_ref[...])
    xn = _rms(x, gf_ref[...]).astype(BF16)
    h = (_silu(_dot(xn, w1_ref[...])) * _dot(xn, w3_ref[...])).astype(BF16)
    x = x + _dot(h, w2_ref[...])
    gate = _sigmoid(_dot(x.astype(BF16), wg_ref[...]))
    emb = _dot(p_ref[...].astype(BF16), wu_ref[...])
    x = x + _rms(gate * emb, gp_ref[...])
    if final:
        x = _rms(x, gfin_ref[...])
    o_ref[...] = x


def _post_mixer(x, act, p, wo, wo_layer, W, layer, final, tm, name):
    m, d = x.shape
    ka = act.shape[1]
    dp = p.shape[-1]
    dff = W['ffn_w1'].shape[-1]
    row = lambda i: (i, 0)
    return pl.pallas_call(
        functools.partial(_post_kernel, final=final),
        grid=(m // tm,),
        in_specs=[pl.BlockSpec((tm, d), row), pl.BlockSpec((tm, ka), row),
                  pl.BlockSpec((None, tm, dp), lambda i: (layer, i, 0)),
                  _resident((ka, d), wo_layer), _resident((1, d), layer), _resident((d, dff), layer),
                  _resident((d, dff), layer), _resident((dff, d), layer), _resident((d, d), layer),
                  _resident((dp, d), layer), _resident((1, d), layer), _resident((1, d))],
        out_specs=pl.BlockSpec((tm, d), row),
        out_shape=jax.ShapeDtypeStruct((m, d), F32),
        compiler_params=_compiler_params(("parallel",)),
        name=name,
    )(x, act, p, wo, W['g_ffn'], W['ffn_w1'], W['ffn_w3'], W['ffn_w2'], W['ple_gate'], W['ple_up'],
      W['g_ple'], W['g_final'])


def _even_kernel(proj_ref, s0_ref, h0_ref, c0_ref, lbp_ref, gn_ref, cw_ref, cb_ref, wa_ref, ba_ref,
                 wx_ref, bx_ref, lam_ref, *rest, nseq, t_len, n_heads, fresh, n_alias):
    act_ref, s_ref, h_ref, c_ref, hist_ref = rest[n_alias:]
    rows = nseq * t_len
    wid = proj_ref.shape[1] // 6
    dk = wid // n_heads
    levels = t_len.bit_length() - 1
    tiles = t_len // SUBLANES
    chunk = pl.program_id(1)

    @pl.when(chunk == 0)
    def _():
        s_ref[...] = s0_ref[...]
        h_ref[...] = h0_ref[...]
        hist_ref[:, :SUBLANES - HIST, :] = jnp.zeros((nseq, SUBLANES - HIST, hist_ref.shape[2]), F32)
        hist_ref[:, SUBLANES - HIST:, :] = c0_ref[...]

    t_in = lax.broadcasted_iota(jnp.int32, (rows, wid), 0) & (t_len - 1)

    u, new_hist = _conv_step(proj_ref[:, 5 * wid:6 * wid], hist_ref, cw_ref[...], cb_ref[...], nseq, t_len)
    c_ref[...] = new_hist
    u16 = u.astype(BF16)
    hs = [slice(h * dk, (h + 1) * dk) for h in range(n_heads)]
    r = _sigmoid(jnp.concatenate([_dot(u16[:, s], wa_ref[h]) for h, s in enumerate(hs)], axis=1) + ba_ref[...])
    gi = _sigmoid(jnp.concatenate([_dot(u16[:, s], wx_ref[h]) for h, s in enumerate(hs)], axis=1) + bx_ref[...])
    log_a = (-LRU_C) * r * _softplus(-lam_ref[...])
    a = jnp.exp(log_a)
    mult = jnp.sqrt(_neg_expm1(2.0 * log_a))
    if fresh:
        mult = jnp.where((t_in == 0) & (chunk == 0), 1.0, mult)
    bt = mult * (gi * u)
    t_sub = t_in & (SUBLANES - 1)
    shift = 1
    while shift < SUBLANES:
        valid = t_sub >= shift
        a_sh = pltpu.roll(a, shift, 0)
        b_sh = pltpu.roll(bt, shift, 0)
        bt = jnp.where(valid, a * b_sh + bt, bt)
        a = jnp.where(valid, a * a_sh, a)
        shift *= 2
    a4 = a.reshape(nseq, tiles, SUBLANES, wid)
    b4 = bt.reshape(nseq, tiles, SUBLANES, wid)
    carry = h_ref[...]
    h_tiles = []
    for k in range(tiles):
        h_k = a4[:, k] * carry + b4[:, k]
        carry = h_k[:, SUBLANES - 1:SUBLANES, :]
        h_tiles.append(h_k)
    h_ref[...] = carry
    hseq = (h_tiles[0] if tiles == 1 else jnp.concatenate(h_tiles, axis=1)).reshape(rows, wid)
    act_ref[:, wid:2 * wid] = (_gelu_tanh(proj_ref[:, 4 * wid:5 * wid]) * hseq).astype(act_ref.dtype)

    lbp = lbp_ref[...]
    fz = proj_ref[:, wid:2 * wid]
    logf = _logaddexp(lbp[0:1, :], lbp[1:2, :] + _log_sigmoid(fz))
    k = lbp[2:3, :] * _sigmoid(-fz)
    q = _silu(proj_ref[:, 0:wid])
    v16 = proj_ref[:, 2 * wid:3 * wid].astype(BF16)
    tri = _seq_tri(rows, t_len)
    b = _dot_exact(tri.astype(BF16), logf)
    b_last = _seq_last(b, nseq, t_len)
    qe16 = (q * jnp.exp(b)).astype(BF16)
    kd16 = (k * jnp.exp(b_last - b)).astype(BF16)

    ql, kl = [q.astype(BF16)], [k.astype(BF16)]
    for lvl in range(1, levels + 1):
        blk = 1 << lvl
        if lvl == 1:
            e = jnp.where((t_in & 1) == 1, logf, 0.0)
        elif lvl == 2:
            nxt = pltpu.roll(logf, rows - 1, 0)
            prv = pltpu.roll(logf, 1, 0)
            pos = t_in & 3
            e = jnp.where(pos == 0, nxt, jnp.where(pos == 1, 0.0, jnp.where(pos == 2, logf, logf + prv)))
        else:
            b3 = b.reshape(rows // blk, blk, wid)
            mid = jnp.broadcast_to(b3[:, blk // 2 - 1:blk // 2, :], b3.shape).reshape(rows, wid)
            e = -jnp.abs(b - mid)
        w = jnp.exp(e)
        ql.append((q * w).astype(BF16))
        kl.append((k * w).astype(BF16))

    r_i = lax.broadcasted_iota(jnp.int32, (rows, rows), 0)
    c_i = lax.broadcasted_iota(jnp.int32, (rows, rows), 1)
    diff = r_i ^ c_i
    pair_level = jnp.where(c_i > r_i, -1, 32 - lax.clz(diff))

    gn = gn_ref[...]
    g_act = _silu(proj_ref[:, 3 * wid:4 * wid])
    for h, s in enumerate(hs):
        att = jnp.zeros((rows, rows), F32)
        for lvl in range(levels + 1):
            att = jnp.where(pair_level == lvl, _dot_nt(ql[lvl][:, s], kl[lvl][:, s]), att)
        o = _dot(att.astype(BF16), v16[:, s])
        inter = []
        for i in range(nseq):
            rs = slice(i * t_len, (i + 1) * t_len)
            s_old = s_ref[i, h]
            inter.append(_dot(qe16[rs, s], s_old.astype(BF16)))
            e_last = jnp.exp(b[(i + 1) * t_len - 1:(i + 1) * t_len, s])
            scale = jnp.transpose(jnp.broadcast_to(e_last, (dk, dk)))
            s_ref[i, h] = scale * s_old + _dot_tn(kd16[rs, s], v16[rs, s])
        o = o + (inter[0] if nseq == 1 else jnp.concatenate(inter, axis=0))
        o = o * lax.rsqrt(jnp.mean(o * o, axis=-1, keepdims=True) + EPS)
        act_ref[:, s] = (o * gn[:, s] * g_act[:, s]).astype(act_ref.dtype)


def _even_mixer(proj, states, prev, W, j, *, n_batch, seq_len, nseq, t_len, fresh, act_dtype, name):
    s0, h0, c0 = states
    n_heads, dk = s0.shape[2], s0.shape[3]
    wid = n_heads * dk
    rows = nseq * t_len
    n_chunks = seq_len // t_len
    blk = lambda b, c: (j, b, 0, 0)
    blk5 = lambda b, c: (j, b, 0, 0, 0)
    n_alias = len(prev)
    kern = functools.partial(_even_kernel, nseq=nseq, t_len=t_len, n_heads=n_heads, fresh=fresh,
                             n_alias=n_alias)
    state_specs = [pl.BlockSpec((None, nseq, n_heads, dk, dk), blk5),
                   pl.BlockSpec((None, nseq, 1, wid), blk),
                   pl.BlockSpec((None, nseq, HIST, wid), blk)]
    n_in = 13
    return pl.pallas_call(
        kern,
        grid=(n_batch // nseq, n_chunks),
        in_specs=[pl.BlockSpec((rows, 6 * wid), lambda b, c: (b * n_chunks + c, 0))] + state_specs + [
                  _resident((3, wid), j), _resident((1, wid), j), _resident((CONV_W, wid), j),
                  _resident((1, wid), j), _resident(W['lru_wa'].shape[1:], j), _resident((1, wid), j),
                  _resident(W['lru_wx'].shape[1:], j), _resident((1, wid), j), _resident((1, wid), j)]
                 + [pl.BlockSpec(memory_space=pl.ANY)] * n_alias,
        out_specs=[pl.BlockSpec((rows, 2 * wid), lambda b, c: (b * n_chunks + c, 0))] + state_specs,
        out_shape=[jax.ShapeDtypeStruct((n_batch * seq_len, 2 * wid), act_dtype),
                   jax.ShapeDtypeStruct(s0.shape, F32),
                   jax.ShapeDtypeStruct(h0.shape, F32),
                   jax.ShapeDtypeStruct(c0.shape, F32)],
        scratch_shapes=[pltpu.VMEM((nseq, SUBLANES, wid), F32)],
        input_output_aliases={n_in + a: 1 + a for a in range(n_alias)},
        compiler_params=_compiler_params(("parallel", "arbitrary")),
        name=name,
    )(proj, s0, h0, c0, W['lbp'], W['hgrn_gnorm'], W['lru_conv_w'], W['lru_conv_b'], W['lru_wa'], W['lru_ba'],
      W['lru_wx'], W['lru_bx'], W['lru_lam'], *prev)


def _odd_kernel(proj_ref, s0_ref, c0_ref, cw_ref, cb_ref, dtb_ref, alog_ref, dx_ref, gn_ref, *rest,
                nseq, t_len, n_heads, n_groups, n_state, n_alias):
    act_ref, s_ref, c_ref, hist_ref, y_ref = rest[n_alias:]
    rows = nseq * t_len
    d_inner = act_ref.shape[1]
    p_dim = d_inner // n_heads
    gw = n_groups * n_state
    hpg = n_heads // n_groups
    gcols = hpg * p_dim
    lanes = 2 * p_dim
    chunk = pl.program_id(1)

    @pl.when(chunk == 0)
    def _():
        s_ref[...] = s0_ref[...]
        hist_ref[:, :SUBLANES - HIST, :] = jnp.zeros((nseq, SUBLANES - HIST, hist_ref.shape[2]), F32)
        hist_ref[:, SUBLANES - HIST:, :] = c0_ref[...]

    conv, new_hist = _conv_step(proj_ref[:, d_inner:2 * d_inner + 2 * gw], hist_ref, cw_ref[...], cb_ref[...],
                                nseq, t_len)
    c_ref[...] = new_hist
    xbc = _silu(conv)
    xs = xbc[:, 0:d_inner]
    b16 = xbc[:, d_inner:d_inner + gw].astype(BF16)
    c16 = xbc[:, d_inner + gw:d_inner + 2 * gw].astype(BF16)

    dt = _softplus(proj_ref[:, 2 * d_inner + 2 * gw:] + dtb_ref[...])
    la = dt * (-jnp.exp(alog_ref[...]))
    tri = _seq_tri(rows, t_len)
    cum = _dot_exact(tri.astype(BF16), la)
    cum_last = _seq_last(cum, nseq, t_len)
    e_cum = jnp.exp(cum)
    w_upd = jnp.exp(cum_last - cum) * dt
    cum_t = jnp.transpose(cum)
    dt_t = jnp.transpose(dt)

    lane = lax.broadcasted_iota(jnp.int32, (rows, lanes), 1)
    low = lane < p_dim

    def pair_cols(x, h):
        return jnp.where(low, jnp.broadcast_to(x[:, h:h + 1], (rows, lanes)),
                         jnp.broadcast_to(x[:, h + 1:h + 2], (rows, lanes)))

    for g in range(n_groups):
        gs = slice(g * n_state, (g + 1) * n_state)
        cb_g = _dot_nt(c16[:, gs], b16[:, gs])
        y_int, x_upd = [], []
        for i in range(nseq):
            rs = slice(i * t_len, (i + 1) * t_len)
            s_old = s_ref[i, g * gcols:(g + 1) * gcols, :]
            y_int.append(_dot_nt(c16[rs, gs], s_old.astype(BF16)))
        y_int = y_int[0] if nseq == 1 else jnp.concatenate(y_int, axis=0)
        for hp in range(hpg // 2):
            h0 = g * hpg + 2 * hp
            cols = slice(h0 * p_dim, (h0 + 2) * p_dim)
            x_pair = xs[:, cols]
            y_pair = jnp.zeros((rows, lanes), F32)
            for sub in range(2):
                h = h0 + sub
                seg = jnp.broadcast_to(cum[:, h:h + 1], (rows, rows)) - jnp.broadcast_to(cum_t[h:h + 1, :], (rows, rows))
                decay = jnp.exp(jnp.where(tri, seg, -jnp.inf))
                m = (cb_g * decay * jnp.broadcast_to(dt_t[h:h + 1, :], (rows, rows))).astype(BF16)
                x_half = jnp.where(low if sub == 0 else ~low, x_pair, 0.0).astype(BF16)
                y_pair = y_pair + _dot(m, x_half)
            y_pair = y_pair + pair_cols(e_cum, h0) * y_int[:, 2 * hp * p_dim:(2 * hp + 2) * p_dim]
            y_pair = y_pair + dx_ref[:, cols] * x_pair
            y_ref[:, cols] = y_pair
            x_upd.append((x_pair * pair_cols(w_upd, h0)).astype(BF16))
        x_upd = jnp.concatenate(x_upd, axis=1)
        for i in range(nseq):
            rs = slice(i * t_len, (i + 1) * t_len)
            upd = _dot_tn(x_upd[rs, :], b16[rs, gs])
            last_t = jnp.transpose(jnp.broadcast_to(cum[(i + 1) * t_len - 1:(i + 1) * t_len, :],
                                                    (cum.shape[1], cum.shape[1])))
            for hl in range(hpg):
                h = g * hpg + hl
                hr = slice(h * p_dim, (h + 1) * p_dim)
                dec = jnp.exp(jnp.broadcast_to(last_t[h:h + 1, :], (p_dim, n_state)))
                s_ref[i, hr, :] = dec * s_ref[i, hr, :] + upd[hl * p_dim:(hl + 1) * p_dim, :]

    z = proj_ref[:, 0:d_inner]
    gn = gn_ref[...]
    norm_w = d_inner // n_groups
    for g in range(n_groups):
        cols = slice(g * norm_w, (g + 1) * norm_w)
        y = y_ref[:, cols] * _silu(z[:, cols])
        y = y * lax.rsqrt(jnp.mean(y * y, axis=-1, keepdims=True) + EPS)
        act_ref[:, cols] = (y * gn[:, cols]).astype(act_ref.dtype)


def _odd_mixer(proj, states, prev, W, j, *, n_batch, seq_len, nseq, t_len, n_heads, act_dtype, name):
    s0, c0 = states
    d_inner, n_state = s0.shape[2], s0.shape[3]
    conv_dim = c0.shape[3]
    rows = nseq * t_len
    n_chunks = seq_len // t_len
    blk = lambda b, c: (j, b, 0, 0)
    n_alias = len(prev)
    kern = functools.partial(_odd_kernel, nseq=nseq, t_len=t_len, n_heads=n_heads, n_groups=W['n_groups'],
                             n_state=n_state, n_alias=n_alias)
    state_specs = [pl.BlockSpec((None, nseq, d_inner, n_state), blk,
                                pipeline_mode=pl.Buffered(1) if nseq > 4 else None),
                   pl.BlockSpec((None, nseq, HIST, conv_dim), blk)]
    n_in = 9
    return pl.pallas_call(
        kern,
        grid=(n_batch // nseq, n_chunks),
        in_specs=[pl.BlockSpec((rows, proj.shape[1]), lambda b, c: (b * n_chunks + c, 0))] + state_specs + [
                  _resident((CONV_W, conv_dim), j), _resident((1, conv_dim), j),
                  _resident(W['ssm_dt_bias'].shape[1:], j), _resident(W['ssm_a_log'].shape[1:], j),
                  _resident((1, d_inner), j), _resident((1, d_inner), j)]
                 + [pl.BlockSpec(memory_space=pl.ANY)] * n_alias,
        out_specs=[pl.BlockSpec((rows, d_inner), lambda b, c: (b * n_chunks + c, 0))] + state_specs,
        out_shape=[jax.ShapeDtypeStruct((n_batch * seq_len, d_inner), act_dtype),
                   jax.ShapeDtypeStruct(s0.shape, F32),
                   jax.ShapeDtypeStruct(c0.shape, F32)],
        scratch_shapes=[pltpu.VMEM((nseq, SUBLANES, conv_dim), F32),
                        pltpu.VMEM((rows, d_inner), F32)],
        input_output_aliases={n_in + a: 1 + a for a in range(n_alias)},
        compiler_params=_compiler_params(("parallel", "arbitrary")),
        name=name,
    )(proj, s0, c0, W['ssm_conv_w'], W['ssm_conv_b'], W['ssm_dt_bias'], W['ssm_a_log'], W['ssm_dx'],
      W['ssm_gnorm'], *prev)


def _trunk(x, p, even_states, odd_states, fresh, W, cfg, tag):
    n_batch, seq_len, d = x.shape
    m = n_batch * seq_len
    depth = p.shape[0]
    nseq, t_len, tm_proj, tm_post, act_dtype = cfg
    x = x.reshape(m, d)
    p = p.reshape(depth, m, p.shape[-1])
    even_out, odd_out = (), ()
    for i in range(depth):
        j = i // 2
        if i % 2 == 0:
            proj = _norm_proj(x, W['g_mix'], W['w_even_in'], i, j, tm_proj, f"{tag}_proj{i}")
            act, *even_out = _even_mixer(proj, even_states, even_out, W, j, n_batch=n_batch, seq_len=seq_len,
                                         nseq=nseq, t_len=t_len, fresh=fresh, act_dtype=act_dtype,
                                         name=f"{tag}_even{i}")
            wo = W['w_even_out']
        else:
            proj = _norm_proj(x, W['g_mix'], W['ssm_in'], i, j, tm_proj, f"{tag}_proj{i}")
            act, *odd_out = _odd_mixer(proj, odd_states, odd_out, W, j, n_batch=n_batch, seq_len=seq_len,
                                       nseq=nseq, t_len=t_len, n_heads=W['n_heads_c'], act_dtype=act_dtype,
                                       name=f"{tag}_odd{i}")
            wo = W['ssm_out']
        x = _post_mixer(x, act, p, wo, j, W, i, i == depth - 1, tm_post, f"{tag}_post{i}")
    return x.reshape(n_batch, seq_len, d), even_out, odd_out


def kernel(x_prompt, x_sample, state_hgrn, state_lru_h, state_lru_conv, state_ssm, state_ssm_conv, p_prompt, p_sample, g_mix, g_ffn, g_ple, g_final, w_even_in, hgrn_lb, hgrn_gnorm, lru_conv_w, lru_conv_b, lru_wa, lru_ba, lru_wx, lru_bx, lru_lam, w_even_out, ssm_in, ssm_conv_w, ssm_conv_b, ssm_dt_bias, ssm_a_log, ssm_d, ssm_gnorm, ssm_out, ffn_w1, ffn_w3, ffn_w2, ple_up, ple_gate):
    n_even, _, n_heads_a, dk_a, dv_a = state_hgrn.shape
    n_odd, _, n_heads_c, p_c, n_c = state_ssm.shape
    wid = state_lru_h.shape[-1]
    d = x_prompt.shape[-1]
    d_inner = n_heads_c * p_c
    conv_dim = state_ssm_conv.shape[-1]
    lane_blk = 128

    lb = jnp.cumsum(jax.nn.softmax(hgrn_lb.astype(F32), axis=0), axis=0)
    lb = lb - lb[0]
    lbp = jnp.stack([jnp.log(lb), jnp.log1p(-lb), 1.0 - lb], axis=1)
    in_c = ssm_in.shape[-1]
    in_pad = -(-in_c // lane_blk) * lane_blk
    head_pad = in_pad - (in_c - n_heads_c)
    pad_h = lambda a: jnp.pad(a.astype(F32), ((0, 0), (0, head_pad - n_heads_c))).reshape(n_odd, 1, head_pad)
    vec = lambda a: a.astype(F32).reshape(a.shape[0], 1, -1)
    W = dict(
        g_mix=vec(g_mix), g_ffn=vec(g_ffn), g_ple=vec(g_ple), g_final=g_final.reshape(1, d),
        lbp=lbp, hgrn_gnorm=vec(hgrn_gnorm), lru_conv_w=lru_conv_w, lru_conv_b=vec(lru_conv_b),
        lru_ba=vec(lru_ba), lru_bx=vec(lru_bx), lru_lam=vec(lru_lam),
        ssm_conv_w=ssm_conv_w, ssm_conv_b=vec(ssm_conv_b), ssm_gnorm=vec(ssm_gnorm),
        ssm_dt_bias=pad_h(ssm_dt_bias), ssm_a_log=pad_h(ssm_a_log),
        ssm_dx=jnp.repeat(ssm_d.astype(F32), p_c, axis=1).reshape(n_odd, 1, d_inner),
        w_even_in=w_even_in.astype(BF16), lru_wa=lru_wa.astype(BF16), lru_wx=lru_wx.astype(BF16),
        w_even_out=w_even_out.astype(BF16),
        ssm_in=jnp.pad(ssm_in.astype(BF16), ((0, 0), (0, 0), (0, in_pad - in_c))),
        ssm_out=ssm_out.astype(BF16), ffn_w1=ffn_w1.astype(BF16), ffn_w3=ffn_w3.astype(BF16),
        ffn_w2=ffn_w2.astype(BF16), ple_up=ple_up.astype(BF16), ple_gate=ple_gate.astype(BF16),
        n_groups=(conv_dim - d_inner) // (2 * n_c), n_heads_c=n_heads_c)

    def run(x, p, hgrn, lru_h, lru_conv, ssm, ssm_conv, fresh, cfg, tag):
        nb = x.shape[0]
        even_states = (hgrn, lru_h.reshape(n_even, nb, 1, wid), lru_conv)
        odd_states = (ssm.reshape(n_odd, nb, d_inner, n_c), ssm_conv)
        y, (hg, lh, lc), (ss, sc) = _trunk(x, p, even_states, odd_states, fresh, W, cfg, tag)
        return y, hg, lh.reshape(n_even, nb, wid), lc, ss.reshape(n_odd, nb, n_heads_c, p_c, n_c), sc

    bp = x_prompt.shape[0]
    zeros = lambda ref: jnp.zeros((ref.shape[0], bp) + ref.shape[2:], F32)
    cfg_prompt = (1, 256, 512, 256, BF16)
    cfg_sample = (16, x_sample.shape[1], 512, 256, F32)
    y_p, hg_p, lh_p, lc_p, ss_p, sc_p = run(
        x_prompt, p_prompt, zeros(state_hgrn), zeros(state_lru_h), zeros(state_lru_conv), zeros(state_ssm),
        zeros(state_ssm_conv), True, cfg_prompt, "prompt")
    y_s, hg_s, lh_s, lc_s, ss_s, sc_s = run(
        x_sample, p_sample, state_hgrn, state_lru_h, state_lru_conv, state_ssm, state_ssm_conv, False,
        cfg_sample, "sample")
    return (y_p, y_s, hg_p, hg_s, lh_p, lh_s, lc_p, lc_s, ss_p, ss_s, sc_p, sc_s)
```

```python
import functools
import math

import jax
import jax.numpy as jnp
from jax import lax
from jax.experimental import pallas as pl
from jax.experimental.pallas import tpu as pltpu

F32 = jnp.float32
BF16 = jnp.bfloat16
EPS = 1e-6
LRU_C = 8.0
CONV_W = 4
HIST = CONV_W - 1
SUBLANES = 8
LANES = 128
VMEM_LIMIT = 56 * 1024 * 1024


def _dot(a, b):
    return jnp.dot(a, b, preferred_element_type=F32)


def _dot_tn(a, b):
    return lax.dot_general(a, b, (((0,), (0,)), ((), ())), preferred_element_type=F32)


def _dot_nt(a, b):
    return lax.dot_general(a, b, (((1,), (1,)), ((), ())), preferred_element_type=F32)


def _split_cat(x, terms):
    parts, rest = [], x
    for _ in range(terms):
        piece = rest.astype(BF16)
        parts.append(piece)
        rest = rest - piece.astype(F32)
    return jnp.concatenate(parts, axis=1)


def _prefix_sum(tri16, x):
    return _dot(jnp.concatenate([tri16, tri16], axis=1), jnp.concatenate(_split_rows(x), axis=0))


def _split_rows(x):
    hi = x.astype(BF16)
    lo = (x - hi.astype(F32)).astype(BF16)
    return [hi, lo]


def _rms(x, g):
    return x * lax.rsqrt(jnp.mean(x * x, axis=-1, keepdims=True) + EPS) * g


def _sigmoid(x):
    return jax.nn.sigmoid(x)


def _silu(x):
    return x * jax.nn.sigmoid(x)


def _softplus(x):
    return jnp.maximum(x, 0.0) + jnp.log1p(jnp.exp(-jnp.abs(x)))


def _log_sigmoid(x):
    return jnp.minimum(x, 0.0) - jnp.log(1.0 + jnp.exp(-jnp.abs(x)))


def _logaddexp(a, b):
    return jnp.maximum(a, b) + jnp.log(1.0 + jnp.exp(-jnp.abs(a - b)))


def _gelu_tanh(x):
    return 0.5 * x * (1.0 + jnp.tanh(math.sqrt(2.0 / math.pi) * (x + 0.044715 * (x * x * x))))


def _seq_tri(rows, t_len):
    shift = t_len.bit_length() - 1
    r = lax.broadcasted_iota(jnp.int32, (rows, rows), 0)
    c = lax.broadcasted_iota(jnp.int32, (rows, rows), 1)
    return ((r >> shift) == (c >> shift)) & (c <= r)


def _seq_last(x, nseq, t_len):
    x3 = x.reshape(nseq, t_len, x.shape[-1])
    last = x3[:, t_len - 1:t_len, :]
    return jnp.broadcast_to(last, x3.shape).reshape(x.shape)


def _init_hist(hist_ref, c0_ref):
    nseq, _, ch = hist_ref.shape
    hist_ref[:, :SUBLANES - HIST, :] = jnp.zeros((nseq, SUBLANES - HIST, ch), F32)
    hist_ref[:, SUBLANES - HIST:, :] = c0_ref[...]


def _conv_step(raw, hist_ref, w, bias, nseq, t_len):
    rows, ch = raw.shape
    raw3 = raw.reshape(nseq, t_len, ch)
    hist = hist_ref[...].reshape(nseq * SUBLANES, ch)
    row = lax.broadcasted_iota(jnp.int32, (nseq, SUBLANES, ch), 1)
    out = raw * w[HIST:HIST + 1, :] + bias
    for j in range(1, CONV_W):
        rolled = pltpu.roll(raw, j, 0).reshape(nseq, t_len, ch)
        hist_j = pltpu.roll(hist, (nseq * SUBLANES + j - SUBLANES) % (nseq * SUBLANES), 0)
        head = jnp.where(row < j, hist_j.reshape(nseq, SUBLANES, ch), rolled[:, 0:SUBLANES, :])
        shifted = head if t_len == SUBLANES else jnp.concatenate([head, rolled[:, SUBLANES:, :]], axis=1)
        out = out + shifted.reshape(rows, ch) * w[HIST - j:HIST - j + 1, :]
    hist_ref[...] = raw3[:, t_len - SUBLANES:, :]
    return out, raw3[:, t_len - HIST:, :]


def _resident(shape, layer=None):
    nd = len(shape)
    if layer is None:
        return pl.BlockSpec(shape, lambda *_: (0,) * nd, pipeline_mode=pl.Buffered(1))
    return pl.BlockSpec((None,) + tuple(shape), lambda *_: (layer,) + (0,) * nd, pipeline_mode=pl.Buffered(1))


def _compiler_params(semantics):
    return pltpu.CompilerParams(dimension_semantics=semantics, vmem_limit_bytes=VMEM_LIMIT)


def _conv_tiling(m, seq_len, tm):
    if tm >= seq_len:
        return tm // seq_len, seq_len, 1
    return 1, tm, seq_len // tm


def _proj_even_kernel(x_ref, g_ref, w_ref, c0_ref, cw_ref, cb_ref, lbp_ref, *rest, nseq, t_len, tiles_per_seq,
                      n_alias):
    q_ref, lf_ref, k_ref, v_ref, ga_ref, yb_ref, u_ref, co_ref, hist_ref = rest[n_alias:]
    wid = q_ref.shape[1]

    @pl.when(pl.program_id(0) % tiles_per_seq == 0)
    def _():
        _init_hist(hist_ref, c0_ref)

    xn = _rms(x_ref[...], g_ref[...]).astype(BF16)
    seg = lambda s: _dot(xn, w_ref[:, s * wid:(s + 1) * wid])
    lbp = lbp_ref[...]
    q_ref[...] = _silu(seg(0))
    fz = seg(1)
    lf_ref[...] = _logaddexp(lbp[0:1, :], lbp[1:2, :] + _log_sigmoid(fz))
    k_ref[...] = lbp[2:3, :] * _sigmoid(-fz)
    v_ref[...] = seg(2).astype(v_ref.dtype)
    ga_ref[...] = _silu(seg(3))
    yb_ref[...] = _gelu_tanh(seg(4))
    u, new_hist = _conv_step(seg(5), hist_ref, cw_ref[...], cb_ref[...], nseq, t_len)
    u_ref[...] = u
    co_ref[...] = new_hist


def _proj_even(x, c0, prev, W, layer, j, *, seq_len, tm, v_dtype, name):
    m, d = x.shape
    wid = c0.shape[-1]
    nseq, t_len, tiles_per_seq = _conv_tiling(m, seq_len, tm)
    row = lambda i: (i, 0)
    state_spec = pl.BlockSpec((None, nseq, HIST, wid), lambda i: (j, i // tiles_per_seq, 0, 0))
    seg = pl.BlockSpec((tm, wid), row)
    n_alias = len(prev)
    kern = functools.partial(_proj_even_kernel, nseq=nseq, t_len=t_len, tiles_per_seq=tiles_per_seq,
                             n_alias=n_alias)
    n_in = 7
    return pl.pallas_call(
        kern,
        grid=(m // tm,),
        in_specs=[pl.BlockSpec((tm, d), row), _resident((1, d), layer), _resident((d, 6 * wid), j), state_spec,
                  _resident((CONV_W, wid), j), _resident((1, wid), j), _resident((3, wid), j)]
                 + [pl.BlockSpec(memory_space=pl.ANY)] * n_alias,
        out_specs=[seg] * 7 + [state_spec],
        out_shape=[jax.ShapeDtypeStruct((m, wid), dt) for dt in (F32, F32, F32, v_dtype, F32, F32, F32)]
                  + [jax.ShapeDtypeStruct(c0.shape, F32)],
        scratch_shapes=[pltpu.VMEM((nseq, SUBLANES, wid), F32)],
        input_output_aliases={n_in + a: 7 + a for a in range(n_alias)},
        compiler_params=_compiler_params(("arbitrary",)),
        name=name,
    )(x, W['g_mix'], W['w_even_in'], c0, W['lru_conv_w'], W['lru_conv_b'], W['lbp'], *prev)


def _proj_odd_kernel(x_ref, g_ref, w_ref, c0_ref, cw_ref, cb_ref, dtb_ref, *rest, nseq, t_len, tiles_per_seq,
                     n_alias):
    zs_ref, xs_ref, b_ref, c_ref, dt_ref, co_ref, hist_ref = rest[n_alias:]
    d_inner = zs_ref.shape[1]
    conv_dim = hist_ref.shape[2]
    bc_w = b_ref.shape[1]

    @pl.when(pl.program_id(0) % tiles_per_seq == 0)
    def _():
        _init_hist(hist_ref, c0_ref)

    xn = _rms(x_ref[...], g_ref[...]).astype(BF16)
    for c0 in range(0, d_inner, bc_w):
        zs_ref[:, c0:c0 + bc_w] = _silu(_dot(xn, w_ref[:, c0:c0 + bc_w]))
    for c0 in range(0, conv_dim, bc_w):
        cols = slice(c0, c0 + bc_w)
        conv, new_hist = _conv_step(_dot(xn, w_ref[:, d_inner + c0:d_inner + c0 + bc_w]), hist_ref.at[:, :, cols],
                                    cw_ref[:, cols], cb_ref[:, cols], nseq, t_len)
        co_ref[:, :, cols] = new_hist
        act = _silu(conv)
        if c0 < d_inner:
            xs_ref[:, cols] = act
        elif c0 == d_inner:
            b_ref[...] = act.astype(b_ref.dtype)
        else:
            c_ref[...] = act.astype(c_ref.dtype)
    dt_ref[...] = _softplus(_dot(xn, w_ref[:, d_inner + conv_dim:]) + dtb_ref[...])


def _proj_odd(x, c0, prev, W, layer, j, *, seq_len, tm, bc_dtype, name):
    m, d = x.shape
    conv_dim = c0.shape[-1]
    n_in_cols = W['ssm_in'].shape[-1]
    head_lanes = W['ssm_dt_bias'].shape[-1]
    d_inner = n_in_cols - conv_dim - head_lanes
    bc_w = (conv_dim - d_inner) // 2
    nseq, t_len, tiles_per_seq = _conv_tiling(m, seq_len, tm)
    row = lambda i: (i, 0)
    state_spec = pl.BlockSpec((None, nseq, HIST, conv_dim), lambda i: (j, i // tiles_per_seq, 0, 0))
    n_alias = len(prev)
    kern = functools.partial(_proj_odd_kernel, nseq=nseq, t_len=t_len, tiles_per_seq=tiles_per_seq,
                             n_alias=n_alias)
    n_in = 7
    return pl.pallas_call(
        kern,
        grid=(m // tm,),
        in_specs=[pl.BlockSpec((tm, d), row), _resident((1, d), layer), _resident((d, n_in_cols), j), state_spec,
                  _resident((CONV_W, conv_dim), j), _resident((1, conv_dim), j), _resident((1, head_lanes), j)]
                 + [pl.BlockSpec(memory_space=pl.ANY)] * n_alias,
        out_specs=[pl.BlockSpec((tm, d_inner), row), pl.BlockSpec((tm, d_inner), row),
                   pl.BlockSpec((tm, bc_w), row), pl.BlockSpec((tm, bc_w), row),
                   pl.BlockSpec((tm, head_lanes), row), state_spec],
        out_shape=[jax.ShapeDtypeStruct((m, d_inner), F32), jax.ShapeDtypeStruct((m, d_inner), F32),
                   jax.ShapeDtypeStruct((m, bc_w), bc_dtype), jax.ShapeDtypeStruct((m, bc_w), bc_dtype),
                   jax.ShapeDtypeStruct((m, head_lanes), F32), jax.ShapeDtypeStruct(c0.shape, F32)],
        scratch_shapes=[pltpu.VMEM((nseq, SUBLANES, conv_dim), F32)],
        input_output_aliases={n_in + a: 5 + a for a in range(n_alias)},
        compiler_params=_compiler_params(("arbitrary",)),
        name=name,
    )(x, W['g_mix'], W['ssm_in'], c0, W['ssm_conv_w'], W['ssm_conv_b'], W['ssm_dt_bias'], *prev)


def _post_kernel(*refs, n_act, final):
    x_ref, p_ref = refs[0], refs[1]
    act_refs = refs[2:2 + n_act]
    wo_refs = refs[2 + n_act:2 + 2 * n_act]
    gf_ref, w1_ref, w3_ref, w2_ref, wg_ref, wu_ref, gp_ref, gfin_ref, o_ref = refs[2 + 2 * n_act:]
    x = x_ref[...]
    for act_ref, wo_ref in zip(act_refs, wo_refs):
        x = x + _dot(act_ref[...].astype(BF16), wo_ref[...])
    xn = _rms(x, gf_ref[...]).astype(BF16)
    h = (_silu(_dot(xn, w1_ref[...])) * _dot(xn, w3_ref[...])).astype(BF16)
    x = x + _dot(h, w2_ref[...])
    gate = _sigmoid(_dot(x.astype(BF16), wg_ref[...]))
    emb = _dot(p_ref[...].astype(BF16), wu_ref[...])
    x = x + _rms(gate * emb, gp_ref[...])
    if final:
        x = _rms(x, gfin_ref[...])
    o_ref[...] = x


def _post_mixer(x, acts, p, wo, wo_layer, W, layer, final, tm, name):
    m, d = x.shape
    dp = p.shape[-1]
    dff = W['ffn_w1'].shape[-1]
    row = lambda i: (i, 0)
    act_specs, wo_specs, off = [], [], 0
    for a in acts:
        ka = a.shape[1]
        act_specs.append(pl.BlockSpec((tm, ka), row))
        wo_specs.append(pl.BlockSpec((None, ka, d), lambda i, blk=off // ka: (wo_layer, blk, 0),
                                     pipeline_mode=pl.Buffered(1)))
        off += ka
    return pl.pallas_call(
        functools.partial(_post_kernel, n_act=len(acts), final=final),
        grid=(m // tm,),
        in_specs=[pl.BlockSpec((tm, d), row), pl.BlockSpec((None, tm, dp), lambda i: (layer, i, 0))]
                 + act_specs + wo_specs + [
                  _resident((1, d), layer), _resident((d, dff), layer), _resident((d, dff), layer),
                  _resident((dff, d), layer), _resident((d, d), layer), _resident((dp, d), layer),
                  _resident((1, d), layer), _resident((1, d))],
        out_specs=pl.BlockSpec((tm, d), row),
        out_shape=jax.ShapeDtypeStruct((m, d), F32),
        compiler_params=_compiler_params(("parallel",)),
        name=name,
    )(x, p, *acts, *([wo] * len(acts)), W['g_ffn'], W['ffn_w1'], W['ffn_w3'], W['ffn_w2'], W['ple_gate'],
      W['ple_up'], W['g_ple'], W['g_final'])


def _even_kernel(q_ref, lf_ref, k_ref, v_ref, ga_ref, yb_ref, u_ref, s0_ref, h0_ref, gn_ref, wa_ref, ba_ref,
                 wx_ref, bx_ref, lam_ref, *rest, nseq, t_len, n_sub, fresh, n_alias):
    oa_ref, ob_ref, s_ref, h_ref = rest[n_alias:]
    rows = nseq * t_len
    wid = q_ref.shape[1]
    levels = t_len.bit_length() - 1
    tiles = t_len // SUBLANES
    step = pl.program_id(2)

    @pl.when(step == 0)
    def _():
        s_ref[...] = s0_ref[...]
        h_ref[...] = h0_ref[...]

    t_in = lax.broadcasted_iota(jnp.int32, (rows, wid), 0) & (t_len - 1)
    t_sub = t_in & (SUBLANES - 1)
    tri16 = _seq_tri(rows, t_len).astype(BF16)
    r_i = lax.broadcasted_iota(jnp.int32, (rows, rows), 0)
    c_i = lax.broadcasted_iota(jnp.int32, (rows, rows), 1)
    pair_level = jnp.where(c_i > r_i, -1, 32 - lax.clz(r_i ^ c_i))
    soft_lam = _softplus(-lam_ref[...])

    for sub in range(n_sub):
        rsl = slice(sub * rows, (sub + 1) * rows)
        u = u_ref[rsl, :]
        u16 = u.astype(BF16)
        r = _sigmoid(_dot(u16, wa_ref[...]) + ba_ref[...])
        gi = _sigmoid(_dot(u16, wx_ref[...]) + bx_ref[...])
        log_a = (-LRU_C) * r * soft_lam
        a = jnp.exp(log_a)
        mult = jnp.sqrt(-jnp.tanh(log_a) * (a * a + 1.0))
        if fresh and sub == 0:
            mult = jnp.where((t_in == 0) & (step == 0), 1.0, mult)
        bt = mult * (gi * u)
        shift = 1
        while shift < SUBLANES:
            valid = t_sub >= shift
            a_sh = pltpu.roll(a, shift, 0)
            b_sh = pltpu.roll(bt, shift, 0)
            bt = jnp.where(valid, a * b_sh + bt, bt)
            a = jnp.where(valid, a * a_sh, a)
            shift *= 2
        a4 = a.reshape(nseq, tiles, SUBLANES, wid)
        b4 = bt.reshape(nseq, tiles, SUBLANES, wid)
        carry = h_ref[...]
        h_tiles = []
        for tile in range(tiles):
            h_k = a4[:, tile] * carry + b4[:, tile]
            carry = h_k[:, SUBLANES - 1:SUBLANES, :]
            h_tiles.append(h_k)
        h_ref[...] = carry
        hseq = (h_tiles[0] if tiles == 1 else jnp.concatenate(h_tiles, axis=1)).reshape(rows, wid)
        ob_ref[rsl, :] = (yb_ref[rsl, :] * hseq).astype(ob_ref.dtype)

        logf = lf_ref[rsl, :]
        k = k_ref[rsl, :]
        q = q_ref[rsl, :]
        v16 = v_ref[rsl, :].astype(BF16)
        b = _prefix_sum(tri16, logf)
        b_last = _seq_last(b, nseq, t_len)
        qe16 = (q * jnp.exp(b)).astype(BF16)
        kd16 = (k * jnp.exp(b_last - b)).astype(BF16)
        q16 = q.astype(BF16)
        k16 = k.astype(BF16)

        att = jnp.where(pair_level == 0, _dot_nt(q16, k16), 0.0)
        for lvl in range(1, levels + 1):
            blk = 1 << lvl
            if lvl == 1:
                e = jnp.where((t_in & 1) == 1, logf, 0.0)
            elif lvl == 2:
                nxt = pltpu.roll(logf, rows - 1, 0)
                prv = pltpu.roll(logf, 1, 0)
                pos = t_in & 3
                e = jnp.where(pos == 0, nxt, jnp.where(pos == 1, 0.0, jnp.where(pos == 2, logf, logf + prv)))
            else:
                b3 = b.reshape(rows // blk, blk, wid)
                mid = jnp.broadcast_to(b3[:, blk // 2 - 1:blk // 2, :], b3.shape).reshape(rows, wid)
                e = -jnp.abs(b - mid)
            w16 = jnp.exp(e).astype(BF16)
            att = jnp.where(pair_level == lvl, _dot_nt(q16 * w16, k16 * w16), att)

        o = _dot(att.astype(BF16), v16)
        inter = []
        for i in range(nseq):
            rs = slice(i * t_len, (i + 1) * t_len)
            s_old = s_ref[i]
            inter.append(_dot(qe16[rs, :], s_old.astype(BF16)))
            e_last = jnp.exp(b[(i + 1) * t_len - 1:(i + 1) * t_len, :])
            scale = jnp.transpose(jnp.broadcast_to(e_last, (wid, wid)))
            s_ref[i] = scale * s_old + _dot_tn(kd16[rs, :], v16[rs, :])
        o = o + (inter[0] if nseq == 1 else jnp.concatenate(inter, axis=0))
        o = o * lax.rsqrt(jnp.mean(o * o, axis=-1, keepdims=True) + EPS)
        oa_ref[rsl, :] = (o * gn_ref[...] * ga_ref[rsl, :]).astype(oa_ref.dtype)


def _even_mixer(segs, states, prev, W, j, *, n_batch, seq_len, nseq, t_len, n_sub, fresh, act_dtype, name):
    s0, h0 = states
    n_heads, dk = s0.shape[2], s0.shape[3]
    wid = n_heads * dk
    rows_step = nseq * t_len * n_sub
    n_steps = seq_len // (t_len * n_sub)
    n_alias = len(prev)
    kern = functools.partial(_even_kernel, nseq=nseq, t_len=t_len, n_sub=n_sub, fresh=fresh, n_alias=n_alias)
    col = pl.BlockSpec((rows_step, dk), lambda b, h, s: (b * n_steps + s, h))
    par = pl.BlockSpec((None, 1, dk), lambda b, h, s: (j, 0, h))
    gate_w = pl.BlockSpec((None, None, dk, dk), lambda b, h, s: (j, h, 0, 0))
    state_specs = [pl.BlockSpec((None, nseq, None, dk, dk), lambda b, h, s: (j, b, h, 0, 0)),
                   pl.BlockSpec((None, nseq, 1, dk), lambda b, h, s: (j, b, 0, h))]
    n_in = 15
    return pl.pallas_call(
        kern,
        grid=(n_batch // nseq, n_heads, n_steps),
        in_specs=[col] * 7 + state_specs + [par, gate_w, par, gate_w, par, par]
                 + [pl.BlockSpec(memory_space=pl.ANY)] * n_alias,
        out_specs=[col, col] + state_specs,
        out_shape=[jax.ShapeDtypeStruct((n_batch * seq_len, wid), act_dtype),
                   jax.ShapeDtypeStruct((n_batch * seq_len, wid), act_dtype),
                   jax.ShapeDtypeStruct(s0.shape, F32),
                   jax.ShapeDtypeStruct(h0.shape, F32)],
        input_output_aliases={n_in + a: 2 + a for a in range(n_alias)},
        compiler_params=_compiler_params(("parallel", "parallel", "arbitrary")),
        name=name,
    )(*segs, s0, h0, W['hgrn_gnorm'], W['lru_wa'], W['lru_ba'], W['lru_wx'], W['lru_bx'], W['lru_lam'], *prev)


def _odd_kernel(zs_ref, xs_ref, bm_ref, cm_ref, dt_ref, s0_ref, alog_ref, dx_ref, gn_ref, ecol_ref, ecols_ref,
                *rest, nseq, t_len, n_sub, n_steps, hpg, n_alias):
    act_ref, s_ref, st_ref = rest[n_alias:]
    carry_t = n_sub * n_steps > 1
    rows = nseq * t_len
    gcols = xs_ref.shape[1]
    p_dim = gcols // hpg
    n_state = bm_ref.shape[1]
    lanes = 2 * p_dim
    group = pl.program_id(1)
    step = pl.program_id(2)

    @pl.when(step == 0)
    def _():
        if carry_t:
            st_ref[...] = jnp.transpose(s0_ref[0])
        else:
            s_ref[...] = s0_ref[...]

    tri = _seq_tri(rows, t_len)
    tri16 = tri.astype(BF16)
    low = lax.broadcasted_iota(jnp.int32, (rows, lanes), 1) < p_dim
    head_lanes = dt_ref.shape[1]
    to_front = lax.rem(head_lanes - group * hpg, head_lanes)
    neg_a = -jnp.exp(alog_ref[...])

    for sub in range(n_sub):
        rsl = slice(sub * rows, (sub + 1) * rows)
        xs = xs_ref[rsl, :]
        b16 = bm_ref[rsl, :].astype(BF16)
        c16 = cm_ref[rsl, :].astype(BF16)
        dt_all = dt_ref[rsl, :]
        dt = pltpu.roll(dt_all, to_front, 1)
        la = pltpu.roll(dt_all * neg_a, to_front, 1)
        cum = _prefix_sum(tri16, la)
        cum_last = _seq_last(cum, nseq, t_len)
        e_cum = jnp.exp(cum)
        w_upd = jnp.exp(cum_last - cum) * dt
        cum_t = jnp.transpose(cum)
        dt_t = jnp.transpose(dt)
        cum_cols = _dot(_split_cat(cum, 3), ecol_ref[...])
        e_cols = _dot(_split_cat(e_cum, 2), ecols_ref[...])
        w_cols = _dot(_split_cat(w_upd, 2), ecols_ref[...])

        cb_g = _dot_nt(c16, b16)
        if carry_t:
            y_int = _dot(c16, st_ref[...].astype(BF16))
        else:
            y_int = []
            for i in range(nseq):
                rs = slice(i * t_len, (i + 1) * t_len)
                y_int.append(_dot_nt(c16[rs, :], s_ref[i].astype(BF16)))
            y_int = y_int[0] if nseq == 1 else jnp.concatenate(y_int, axis=0)
        x_upd = (xs * w_cols).astype(BF16)
        y_cols = []
        for hp in range(hpg // 2):
            cols = slice(2 * hp * p_dim, (2 * hp + 2) * p_dim)
            x_pair = xs[:, cols]
            y_pair = e_cols[:, cols] * y_int[:, cols] + dx_ref[:, cols] * x_pair
            for half in range(2):
                h = 2 * hp + half
                seg = cum_cols[:, h * rows:(h + 1) * rows] - jnp.broadcast_to(cum_t[h:h + 1, :], (rows, rows))
                decay = jnp.exp(jnp.where(tri, seg, -jnp.inf))
                m = (cb_g * (decay * jnp.broadcast_to(dt_t[h:h + 1, :], (rows, rows)))).astype(BF16)
                x_half = jnp.where(low if half == 0 else ~low, x_pair, 0.0).astype(BF16)
                y_pair = y_pair + _dot(m, x_half)
            y_cols.append(y_pair)
        if carry_t:
            st_ref[...] = st_ref[...] * e_cols[rows - 1:rows, :] + _dot_tn(b16, x_upd)
        for i in range(0 if carry_t else nseq):
            rs = slice(i * t_len, (i + 1) * t_len)
            upd = _dot_tn(x_upd[rs, :], b16[rs, :])
            last_t = jnp.transpose(jnp.broadcast_to(cum[(i + 1) * t_len - 1:(i + 1) * t_len, :],
                                                    (head_lanes, head_lanes)))
            for h in range(hpg):
                hr = slice(h * p_dim, (h + 1) * p_dim)
                dec = jnp.exp(jnp.broadcast_to(last_t[h:h + 1, 0:n_state], (p_dim, n_state)))
                s_ref[i, hr, :] = dec * s_ref[i, hr, :] + upd[hr, :]

        y = jnp.concatenate(y_cols, axis=1) * zs_ref[rsl, :]
        y = y * lax.rsqrt(jnp.mean(y * y, axis=-1, keepdims=True) + EPS)
        act_ref[rsl, :] = (y * gn_ref[...]).astype(act_ref.dtype)

    if carry_t:
        @pl.when(step == n_steps - 1)
        def _():
            s_ref[0] = jnp.transpose(st_ref[...])


def _odd_mixer(segs, s0, prev, W, j, *, n_batch, seq_len, nseq, t_len, n_sub, n_heads, act_dtype, name):
    zs, xs, bm, cm, dt = segs
    d_inner, n_state = s0.shape[2], s0.shape[3]
    n_groups = bm.shape[1] // n_state
    hpg = n_heads // n_groups
    gcols = d_inner // n_groups
    head_lanes = dt.shape[1]
    rows = nseq * t_len
    rows_step = rows * n_sub
    n_steps = seq_len // (t_len * n_sub)
    n_alias = len(prev)
    assert nseq == 1 or n_sub * n_steps == 1, "several sequences per step are swept in one sub-chunk"
    kern = functools.partial(_odd_kernel, nseq=nseq, t_len=t_len, n_sub=n_sub, n_steps=n_steps, hpg=hpg,
                             n_alias=n_alias)
    col = lambda width: pl.BlockSpec((rows_step, width), lambda b, g, s: (b * n_steps + s, g))
    par = pl.BlockSpec((None, 1, gcols), lambda b, g, s: (j, 0, g))
    state_spec = pl.BlockSpec((None, nseq, gcols, n_state), lambda b, g, s: (j, b, g, 0))
    head_eye = jnp.eye(head_lanes, hpg, dtype=BF16)
    head_to_rows = jnp.tile(jnp.repeat(head_eye, rows, axis=1), (3, 1))
    head_to_cols = jnp.tile(jnp.repeat(head_eye, gcols // hpg, axis=1), (2, 1))
    n_in = 11
    return pl.pallas_call(
        kern,
        grid=(n_batch // nseq, n_groups, n_steps),
        in_specs=[col(gcols), col(gcols), col(n_state), col(n_state),
                  pl.BlockSpec((rows_step, head_lanes), lambda b, g, s: (b * n_steps + s, 0)), state_spec,
                  _resident((1, head_lanes), j), par, par, _resident(head_to_rows.shape),
                  _resident(head_to_cols.shape)]
                 + [pl.BlockSpec(memory_space=pl.ANY)] * n_alias,
        out_specs=[col(gcols), state_spec],
        out_shape=[jax.ShapeDtypeStruct((n_batch * seq_len, d_inner), act_dtype),
                   jax.ShapeDtypeStruct(s0.shape, F32)],
        scratch_shapes=[pltpu.VMEM((n_state, gcols), F32)],
        input_output_aliases={n_in + a: 1 + a for a in range(n_alias)},
        compiler_params=_compiler_params(("parallel", "parallel", "arbitrary")),
        name=name,
    )(zs, xs, bm, cm, dt, s0, W['ssm_a_log'], W['ssm_dx'], W['ssm_gnorm'], head_to_rows, head_to_cols, *prev)


def _trunk(x, p, even_states, odd_states, fresh, W, cfg, tag):
    n_batch, seq_len, d = x.shape
    m = n_batch * seq_len
    depth = p.shape[0]
    nseq, t_len, n_sub, tm_even, tm_odd, tm_post, act_dtype = cfg
    hgrn0, lru_h0, lru_conv0 = even_states
    ssm0, ssm_conv0 = odd_states
    x = x.reshape(m, d)
    p = p.reshape(depth, m, p.shape[-1])
    mixer_args = dict(n_batch=n_batch, seq_len=seq_len, nseq=nseq, t_len=t_len, n_sub=n_sub, act_dtype=act_dtype)
    even_out, even_conv, odd_out, odd_conv = (), (), (), ()
    for i in range(depth):
        j = i // 2
        if i % 2 == 0:
            *segs, conv = _proj_even(x, lru_conv0, even_conv, W, i, j, seq_len=seq_len, tm=tm_even,
                                     v_dtype=act_dtype, name=f"{tag}_proj{i}")
            even_conv = (conv,)
            act_a, act_b, *even_out = _even_mixer(segs, (hgrn0, lru_h0), even_out, W, j, fresh=fresh,
                                                  name=f"{tag}_even{i}", **mixer_args)
            acts, wo = (act_a, act_b), W['w_even_out']
        else:
            *segs, conv = _proj_odd(x, ssm_conv0, odd_conv, W, i, j, seq_len=seq_len, tm=tm_odd,
                                    bc_dtype=act_dtype, name=f"{tag}_proj{i}")
            odd_conv = (conv,)
            act, *odd_out = _odd_mixer(segs, ssm0, odd_out, W, j, n_heads=W['n_heads_c'],
                                       name=f"{tag}_odd{i}", **mixer_args)
            acts, wo = (act,), W['ssm_out']
        x = _post_mixer(x, acts, p, wo, j, W, i, i == depth - 1, tm_post, f"{tag}_post{i}")
    return x.reshape(n_batch, seq_len, d), even_out, even_conv[0], odd_out[0], odd_conv[0]


def kernel(x_prompt, x_sample, state_hgrn, state_lru_h, state_lru_conv, state_ssm, state_ssm_conv, p_prompt, p_sample, g_mix, g_ffn, g_ple, g_final, w_even_in, hgrn_lb, hgrn_gnorm, lru_conv_w, lru_conv_b, lru_wa, lru_ba, lru_wx, lru_bx, lru_lam, w_even_out, ssm_in, ssm_conv_w, ssm_conv_b, ssm_dt_bias, ssm_a_log, ssm_d, ssm_gnorm, ssm_out, ffn_w1, ffn_w3, ffn_w2, ple_up, ple_gate):
    n_even = state_hgrn.shape[0]
    n_odd, _, n_heads_c, p_c, n_c = state_ssm.shape
    wid = state_lru_h.shape[-1]
    d = x_prompt.shape[-1]
    d_inner = n_heads_c * p_c

    lb = jnp.cumsum(jax.nn.softmax(hgrn_lb.astype(F32), axis=0), axis=0)
    lb = lb - lb[0]
    lbp = jnp.stack([jnp.log(lb), jnp.log1p(-lb), 1.0 - lb], axis=1)
    in_c = ssm_in.shape[-1]
    in_pad = -(-in_c // LANES) * LANES
    head_pad = in_pad - (in_c - n_heads_c)
    pad_h = lambda a: jnp.pad(a.astype(F32), ((0, 0), (0, head_pad - n_heads_c))).reshape(n_odd, 1, head_pad)
    vec = lambda a: a.astype(F32).reshape(a.shape[0], 1, -1)
    W = dict(
        g_mix=vec(g_mix), g_ffn=vec(g_ffn), g_ple=vec(g_ple), g_final=g_final.reshape(1, d),
        lbp=lbp, hgrn_gnorm=vec(hgrn_gnorm), lru_conv_w=lru_conv_w, lru_conv_b=vec(lru_conv_b),
        lru_ba=vec(lru_ba), lru_bx=vec(lru_bx), lru_lam=vec(lru_lam),
        ssm_conv_w=ssm_conv_w, ssm_conv_b=vec(ssm_conv_b), ssm_gnorm=vec(ssm_gnorm),
        ssm_dt_bias=pad_h(ssm_dt_bias), ssm_a_log=pad_h(ssm_a_log),
        ssm_dx=jnp.repeat(ssm_d.astype(F32), p_c, axis=1).reshape(n_odd, 1, d_inner),
        w_even_in=w_even_in.astype(BF16), lru_wa=lru_wa.astype(BF16), lru_wx=lru_wx.astype(BF16),
        w_even_out=w_even_out.astype(BF16),
        ssm_in=jnp.pad(ssm_in.astype(BF16), ((0, 0), (0, 0), (0, in_pad - in_c))),
        ssm_out=ssm_out.astype(BF16), ffn_w1=ffn_w1.astype(BF16), ffn_w3=ffn_w3.astype(BF16),
        ffn_w2=ffn_w2.astype(BF16), ple_up=ple_up.astype(BF16), ple_gate=ple_gate.astype(BF16),
        n_heads_c=n_heads_c)

    def run(x, p, hgrn, lru_h, lru_conv, ssm, ssm_conv, fresh, cfg, tag):
        nb = x.shape[0]
        even_states = (hgrn, lru_h.reshape(n_even, nb, 1, wid), lru_conv)
        odd_states = (ssm.reshape(n_odd, nb, d_inner, n_c), ssm_conv)
        y, (hg, lh), lc, ss, sc = _trunk(x, p, even_states, odd_states, fresh, W, cfg, tag)
        return y, hg, lh.reshape(n_even, nb, wid), lc, ss.reshape(n_odd, nb, n_heads_c, p_c, n_c), sc

    bp = x_prompt.shape[0]
    zeros = lambda ref: jnp.zeros((ref.shape[0], bp) + ref.shape[2:], F32)
    cfg_prompt = (1, 128, 8, 512, 256, 256, BF16)
    cfg_sample = (16, x_sample.shape[1], 1, 512, 256, 256, F32)
    y_p, hg_p, lh_p, lc_p, ss_p, sc_p = run(
        x_prompt, p_prompt, zeros(state_hgrn), zeros(state_lru_h), zeros(state_lru_conv), zeros(state_ssm),
        zeros(state_ssm_conv), True, cfg_prompt, "prompt")
    y_s, hg_s, lh_s, lc_s, ss_s, sc_s = run(
        x_sample, p_sample, state_hgrn, state_lru_h, state_lru_conv, state_ssm, state_ssm_conv, False,
        cfg_sample, "sample")
    return (y_p, y_s, hg_p, hg_s, lh_p, lh_s, lc_p, lc_s, ss_p, ss_s, sc_p, sc_s)
```

```python
import functools
import math

import jax
import jax.numpy as jnp
from jax import lax
from jax.experimental import pallas as pl
from jax.experimental.pallas import tpu as pltpu

F32 = jnp.float32
BF16 = jnp.bfloat16
EPS = 1e-6
LRU_C = 8.0
CONV_W = 4
HIST = CONV_W - 1
SUBLANES = 8
LANES = 128
VMEM_LIMIT = 56 * 1024 * 1024
POST_ROWS = 256


def _dot(a, b):
    return jnp.dot(a, b, preferred_element_type=F32)


def _dot_tn(a, b):
    return lax.dot_general(a, b, (((0,), (0,)), ((), ())), preferred_element_type=F32)


def _dot_nt(a, b):
    return lax.dot_general(a, b, (((1,), (1,)), ((), ())), preferred_element_type=F32)


def _split_cat(x, terms):
    parts, rest = [], x
    for _ in range(terms):
        piece = rest.astype(BF16)
        parts.append(piece)
        rest = rest - piece.astype(F32)
    return jnp.concatenate(parts, axis=1)


def _prefix_sum(tri16, x):
    return _dot(jnp.concatenate([tri16, tri16], axis=1), jnp.concatenate(_split_rows(x), axis=0))


def _split_rows(x):
    hi = x.astype(BF16)
    lo = (x - hi.astype(F32)).astype(BF16)
    return [hi, lo]


def _rms(x, g):
    return x * lax.rsqrt(jnp.mean(x * x, axis=-1, keepdims=True) + EPS) * g


def _sigmoid(x):
    return 0.5 * jnp.tanh(0.5 * x) + 0.5


def _silu(x):
    half = 0.5 * x
    return half * jnp.tanh(half) + half


def _softplus(x):
    return jnp.maximum(x, 0.0) + jnp.log1p(jnp.exp(-jnp.abs(x)))


def _log_sigmoid(x):
    return jnp.minimum(x, 0.0) - jnp.log(1.0 + jnp.exp(-jnp.abs(x)))


def _logaddexp(a, b):
    return jnp.maximum(a, b) + jnp.log(1.0 + jnp.exp(-jnp.abs(a - b)))


def _gelu_tanh(x):
    return 0.5 * x * (1.0 + jnp.tanh(math.sqrt(2.0 / math.pi) * (x + 0.044715 * (x * x * x))))


def _seq_tri(rows, t_len):
    shift = t_len.bit_length() - 1
    r = lax.broadcasted_iota(jnp.int32, (rows, rows), 0)
    c = lax.broadcasted_iota(jnp.int32, (rows, rows), 1)
    return ((r >> shift) == (c >> shift)) & (c <= r)


def _seq_last(x, nseq, t_len):
    x3 = x.reshape(nseq, t_len, x.shape[-1])
    last = x3[:, t_len - 1:t_len, :]
    return jnp.broadcast_to(last, x3.shape).reshape(x.shape)


def _init_hist(hist_ref, c0_ref):
    nseq, _, ch = hist_ref.shape
    hist_ref[:, :SUBLANES - HIST, :] = jnp.zeros((nseq, SUBLANES - HIST, ch), F32)
    hist_ref[:, SUBLANES - HIST:, :] = c0_ref[...]


def _conv_step(raw, hist_ref, w, bias, nseq, t_len):
    rows, ch = raw.shape
    raw3 = raw.reshape(nseq, t_len, ch)
    hist = hist_ref[...].reshape(nseq * SUBLANES, ch)
    row = lax.broadcasted_iota(jnp.int32, (nseq, SUBLANES, ch), 1)
    out = raw * w[HIST:HIST + 1, :] + bias
    for j in range(1, CONV_W):
        rolled = pltpu.roll(raw, j, 0).reshape(nseq, t_len, ch)
        hist_j = pltpu.roll(hist, (nseq * SUBLANES + j - SUBLANES) % (nseq * SUBLANES), 0)
        head = jnp.where(row < j, hist_j.reshape(nseq, SUBLANES, ch), rolled[:, 0:SUBLANES, :])
        shifted = head if t_len == SUBLANES else jnp.concatenate([head, rolled[:, SUBLANES:, :]], axis=1)
        out = out + shifted.reshape(rows, ch) * w[HIST - j:HIST - j + 1, :]
    hist_ref[...] = raw3[:, t_len - SUBLANES:, :]
    return out, raw3[:, t_len - HIST:, :]


def _resident(shape, layer=None):
    nd = len(shape)
    if layer is None:
        return pl.BlockSpec(shape, lambda *_: (0,) * nd, pipeline_mode=pl.Buffered(1))
    return pl.BlockSpec((None,) + tuple(shape), lambda *_: (layer,) + (0,) * nd, pipeline_mode=pl.Buffered(1))


def _compiler_params(semantics):
    return pltpu.CompilerParams(dimension_semantics=semantics, vmem_limit_bytes=VMEM_LIMIT)


def _conv_tiling(m, seq_len, tm):
    if tm >= seq_len:
        return tm // seq_len, seq_len, 1
    return 1, tm, seq_len // tm


def _proj_even_kernel(x_ref, g_ref, w_ref, c0_ref, cw_ref, cb_ref, lbp_ref, *rest, nseq, t_len, tiles_per_seq,
                      n_alias):
    q_ref, lf_ref, k_ref, v_ref, ga_ref, yb_ref, u_ref, co_ref, hist_ref = rest[n_alias:]
    wid = q_ref.shape[1]

    @pl.when(pl.program_id(0) % tiles_per_seq == 0)
    def _():
        _init_hist(hist_ref, c0_ref)

    lbp = lbp_ref[...]
    xn = _rms(x_ref[...], g_ref[...]).astype(BF16)
    seg = lambda s: _dot(xn, w_ref[:, s * wid:(s + 1) * wid])
    q_ref[...] = _silu(seg(0))
    fz = seg(1)
    lf_ref[...] = _logaddexp(lbp[0:1, :], lbp[1:2, :] + _log_sigmoid(fz))
    k_ref[...] = lbp[2:3, :] * _sigmoid(-fz)
    v_ref[...] = seg(2).astype(v_ref.dtype)
    ga_ref[...] = _silu(seg(3))
    yb_ref[...] = _gelu_tanh(seg(4))
    u, new_hist = _conv_step(seg(5), hist_ref, cw_ref[...], cb_ref[...], nseq, t_len)
    u_ref[...] = u
    co_ref[...] = new_hist


def _proj_even(x, c0, prev, W, layer, j, *, seq_len, tm, v_dtype, name):
    m, d = x.shape
    wid = c0.shape[-1]
    nseq, t_len, tiles_per_seq = _conv_tiling(m, seq_len, tm)
    row = lambda i: (i, 0)
    state_spec = pl.BlockSpec((None, nseq, HIST, wid), lambda i: (j, i // tiles_per_seq, 0, 0))
    seg = pl.BlockSpec((tm, wid), row)
    n_alias = len(prev)
    kern = functools.partial(_proj_even_kernel, nseq=nseq, t_len=t_len, tiles_per_seq=tiles_per_seq,
                             n_alias=n_alias)
    n_in = 7
    return pl.pallas_call(
        kern,
        grid=(m // tm,),
        in_specs=[pl.BlockSpec((tm, d), row), _resident((1, d), layer), _resident((d, 6 * wid), j), state_spec,
                  _resident((CONV_W, wid), j), _resident((1, wid), j), _resident((3, wid), j)]
                 + [pl.BlockSpec(memory_space=pl.ANY)] * n_alias,
        out_specs=[seg] * 7 + [state_spec],
        out_shape=[jax.ShapeDtypeStruct((m, wid), dt) for dt in (F32, F32, F32, v_dtype, F32, F32, F32)]
                  + [jax.ShapeDtypeStruct(c0.shape, F32)],
        scratch_shapes=[pltpu.VMEM((nseq, SUBLANES, wid), F32)],
        input_output_aliases={n_in + a: 7 + a for a in range(n_alias)},
        compiler_params=_compiler_params(("arbitrary",)),
        name=name,
    )(x, W['g_mix'], W['w_even_in'], c0, W['lru_conv_w'], W['lru_conv_b'], W['lbp'], *prev)


def _proj_odd_kernel(x_ref, g_ref, w_ref, c0_ref, cw_ref, cb_ref, dtb_ref, *rest, nseq, t_len, tiles_per_seq,
                     n_alias):
    zs_ref, xs_ref, b_ref, c_ref, dt_ref, co_ref, hist_ref = rest[n_alias:]
    d_inner = zs_ref.shape[1]
    conv_dim = hist_ref.shape[2]
    bc_w = b_ref.shape[1]

    @pl.when(pl.program_id(0) % tiles_per_seq == 0)
    def _():
        _init_hist(hist_ref, c0_ref)

    xn = _rms(x_ref[...], g_ref[...]).astype(BF16)
    for c0 in range(0, d_inner, bc_w):
        zs_ref[:, c0:c0 + bc_w] = _silu(_dot(xn, w_ref[:, c0:c0 + bc_w]))
    for c0 in range(0, conv_dim, bc_w):
        cols = slice(c0, c0 + bc_w)
        conv, new_hist = _conv_step(_dot(xn, w_ref[:, d_inner + c0:d_inner + c0 + bc_w]), hist_ref.at[:, :, cols],
                                    cw_ref[:, cols], cb_ref[:, cols], nseq, t_len)
        co_ref[:, :, cols] = new_hist
        act = _silu(conv)
        if c0 < d_inner:
            xs_ref[:, cols] = act
        elif c0 == d_inner:
            b_ref[...] = act.astype(b_ref.dtype)
        else:
            c_ref[...] = act.astype(c_ref.dtype)
    dt_ref[...] = _softplus(_dot(xn, w_ref[:, d_inner + conv_dim:]) + dtb_ref[...])


def _proj_odd(x, c0, prev, W, layer, j, *, seq_len, tm, bc_dtype, name):
    m, d = x.shape
    conv_dim = c0.shape[-1]
    n_in_cols = W['ssm_in'].shape[-1]
    head_lanes = W['ssm_dt_bias'].shape[-1]
    d_inner = n_in_cols - conv_dim - head_lanes
    bc_w = (conv_dim - d_inner) // 2
    nseq, t_len, tiles_per_seq = _conv_tiling(m, seq_len, tm)
    row = lambda i: (i, 0)
    state_spec = pl.BlockSpec((None, nseq, HIST, conv_dim), lambda i: (j, i // tiles_per_seq, 0, 0))
    n_alias = len(prev)
    kern = functools.partial(_proj_odd_kernel, nseq=nseq, t_len=t_len, tiles_per_seq=tiles_per_seq,
                             n_alias=n_alias)
    n_in = 7
    return pl.pallas_call(
        kern,
        grid=(m // tm,),
        in_specs=[pl.BlockSpec((tm, d), row), _resident((1, d), layer), _resident((d, n_in_cols), j), state_spec,
                  _resident((CONV_W, conv_dim), j), _resident((1, conv_dim), j), _resident((1, head_lanes), j)]
                 + [pl.BlockSpec(memory_space=pl.ANY)] * n_alias,
        out_specs=[pl.BlockSpec((tm, d_inner), row), pl.BlockSpec((tm, d_inner), row),
                   pl.BlockSpec((tm, bc_w), row), pl.BlockSpec((tm, bc_w), row),
                   pl.BlockSpec((tm, head_lanes), row), state_spec],
        out_shape=[jax.ShapeDtypeStruct((m, d_inner), F32), jax.ShapeDtypeStruct((m, d_inner), F32),
                   jax.ShapeDtypeStruct((m, bc_w), bc_dtype), jax.ShapeDtypeStruct((m, bc_w), bc_dtype),
                   jax.ShapeDtypeStruct((m, head_lanes), F32), jax.ShapeDtypeStruct(c0.shape, F32)],
        scratch_shapes=[pltpu.VMEM((nseq, SUBLANES, conv_dim), F32)],
        input_output_aliases={n_in + a: 5 + a for a in range(n_alias)},
        compiler_params=_compiler_params(("arbitrary",)),
        name=name,
    )(x, W['g_mix'], W['ssm_in'], c0, W['ssm_conv_w'], W['ssm_conv_b'], W['ssm_dt_bias'], *prev)


def _post_kernel(*refs, n_act, final):
    x_ref, p_ref = refs[0], refs[1]
    act_refs = refs[2:2 + n_act]
    wo_refs = refs[2 + n_act:2 + 2 * n_act]
    gf_ref, w1_ref, w3_ref, w2_ref, wg_ref, wu_ref, gp_ref, gfin_ref, o_ref = refs[2 + 2 * n_act:]
    tm = x_ref.shape[0]
    rb = min(tm, POST_ROWS)
    for r0 in range(0, tm, rb):
        rs = slice(r0, r0 + rb)
        x = x_ref[rs, :]
        for act_ref, wo_ref in zip(act_refs, wo_refs):
            x = x + _dot(act_ref[rs, :].astype(BF16), wo_ref[...])
        xn = _rms(x, gf_ref[...]).astype(BF16)
        h = (_silu(_dot(xn, w1_ref[...])) * _dot(xn, w3_ref[...])).astype(BF16)
        x = x + _dot(h, w2_ref[...])
        gate = _sigmoid(_dot(x.astype(BF16), wg_ref[...]))
        emb = _dot(p_ref[rs, :].astype(BF16), wu_ref[...])
        x = x + _rms(gate * emb, gp_ref[...])
        if final:
            x = _rms(x, gfin_ref[...])
        o_ref[rs, :] = x


def _post_mixer(x, acts, p, wo, wo_layer, W, layer, final, tm, name):
    m, d = x.shape
    dp = p.shape[-1]
    dff = W['ffn_w1'].shape[-1]
    row = lambda i: (i, 0)
    act_specs, wo_specs, off = [], [], 0
    for a in acts:
        ka = a.shape[1]
        act_specs.append(pl.BlockSpec((tm, ka), row))
        wo_specs.append(pl.BlockSpec((None, ka, d), lambda i, blk=off // ka: (wo_layer, blk, 0),
                                     pipeline_mode=pl.Buffered(1)))
        off += ka
    return pl.pallas_call(
        functools.partial(_post_kernel, n_act=len(acts), final=final),
        grid=(m // tm,),
        in_specs=[pl.BlockSpec((tm, d), row), pl.BlockSpec((None, tm, dp), lambda i: (layer, i, 0))]
                 + act_specs + wo_specs + [
                  _resident((1, d), layer), _resident((d, dff), layer), _resident((d, dff), layer),
                  _resident((dff, d), layer), _resident((d, d), layer), _resident((dp, d), layer),
                  _resident((1, d), layer), _resident((1, d))],
        out_specs=pl.BlockSpec((tm, d), row),
        out_shape=jax.ShapeDtypeStruct((m, d), F32),
        compiler_params=_compiler_params(("parallel",)),
        name=name,
    )(x, p, *acts, *([wo] * len(acts)), W['g_ffn'], W['ffn_w1'], W['ffn_w3'], W['ffn_w2'], W['ple_gate'],
      W['ple_up'], W['g_ple'], W['g_final'])


def _even_kernel(q_ref, lf_ref, k_ref, v_ref, ga_ref, yb_ref, u_ref, s0_ref, h0_ref, gn_ref, wa_ref, ba_ref,
                 wx_ref, bx_ref, lam_ref, *rest, nseq, t_len, n_sub, fresh, n_alias):
    oa_ref, ob_ref, s_ref, h_ref = rest[n_alias:]
    rows = nseq * t_len
    wid = q_ref.shape[1]
    levels = t_len.bit_length() - 1
    tiles = t_len // SUBLANES
    step = pl.program_id(2)

    @pl.when(step == 0)
    def _():
        s_ref[...] = s0_ref[...]
        h_ref[...] = h0_ref[...]

    t_in = lax.broadcasted_iota(jnp.int32, (rows, wid), 0) & (t_len - 1)
    t_sub = t_in & (SUBLANES - 1)
    tri16 = _seq_tri(rows, t_len).astype(BF16)
    r_i = lax.broadcasted_iota(jnp.int32, (rows, rows), 0)
    c_i = lax.broadcasted_iota(jnp.int32, (rows, rows), 1)
    pair_level = jnp.where(c_i > r_i, -1, 32 - lax.clz(r_i ^ c_i))
    soft_lam = _softplus(-lam_ref[...])

    for sub in range(n_sub):
        rsl = slice(sub * rows, (sub + 1) * rows)
        u = u_ref[rsl, :]
        u16 = u.astype(BF16)
        r = _sigmoid(_dot(u16, wa_ref[...]) + ba_ref[...])
        gi = _sigmoid(_dot(u16, wx_ref[...]) + bx_ref[...])
        log_a = (-LRU_C) * r * soft_lam
        a = jnp.exp(log_a)
        mult = jnp.sqrt(-jnp.tanh(log_a) * (a * a + 1.0))
        if fresh and sub == 0:
            mult = jnp.where((t_in == 0) & (step == 0), 1.0, mult)
        bt = mult * (gi * u)
        shift = 1
        while shift < SUBLANES:
            valid = t_sub >= shift
            a_sh = pltpu.roll(a, shift, 0)
            b_sh = pltpu.roll(bt, shift, 0)
            bt = jnp.where(valid, a * b_sh + bt, bt)
            a = jnp.where(valid, a * a_sh, a)
            shift *= 2
        a4 = a.reshape(nseq, tiles, SUBLANES, wid)
        b4 = bt.reshape(nseq, tiles, SUBLANES, wid)
        carry = h_ref[...]
        h_tiles = []
        for tile in range(tiles):
            h_k = a4[:, tile] * carry + b4[:, tile]
            carry = h_k[:, SUBLANES - 1:SUBLANES, :]
            h_tiles.append(h_k)
        h_ref[...] = carry
        hseq = (h_tiles[0] if tiles == 1 else jnp.concatenate(h_tiles, axis=1)).reshape(rows, wid)
        ob_ref[rsl, :] = (yb_ref[rsl, :] * hseq).astype(ob_ref.dtype)

        logf = lf_ref[rsl, :]
        k = k_ref[rsl, :]
        q = q_ref[rsl, :]
        v16 = v_ref[rsl, :].astype(BF16)
        b = _prefix_sum(tri16, logf)
        b_last = _seq_last(b, nseq, t_len)
        qe16 = (q * jnp.exp(b)).astype(BF16)
        kd16 = (k * jnp.exp(b_last - b)).astype(BF16)
        q16 = q.astype(BF16)
        k16 = k.astype(BF16)

        att = jnp.where(pair_level == 0, _dot_nt(q16, k16), 0.0)
        for lvl in range(1, levels + 1):
            blk = 1 << lvl
            if lvl == 1:
                e = jnp.where((t_in & 1) == 1, logf, 0.0)
            elif lvl == 2:
                nxt = pltpu.roll(logf, rows - 1, 0)
                prv = pltpu.roll(logf, 1, 0)
                pos = t_in & 3
                e = jnp.where(pos == 0, nxt, jnp.where(pos == 1, 0.0, jnp.where(pos == 2, logf, logf + prv)))
            else:
                b3 = b.reshape(rows // blk, blk, wid)
                mid = jnp.broadcast_to(b3[:, blk // 2 - 1:blk // 2, :], b3.shape).reshape(rows, wid)
                e = -jnp.abs(b - mid)
            w16 = jnp.exp(e).astype(BF16)
            att = jnp.where(pair_level == lvl, _dot_nt(q16 * w16, k16 * w16), att)

        o = _dot(att.astype(BF16), v16)
        inter = []
        for i in range(nseq):
            rs = slice(i * t_len, (i + 1) * t_len)
            s_old = s_ref[i]
            inter.append(_dot(qe16[rs, :], s_old.astype(BF16)))
            e_last = jnp.exp(b[(i + 1) * t_len - 1:(i + 1) * t_len, :])
            scale = jnp.transpose(jnp.broadcast_to(e_last, (wid, wid)))
            s_ref[i] = scale * s_old + _dot_tn(kd16[rs, :], v16[rs, :])
        o = o + (inter[0] if nseq == 1 else jnp.concatenate(inter, axis=0))
        o = o * lax.rsqrt(jnp.mean(o * o, axis=-1, keepdims=True) + EPS)
        oa_ref[rsl, :] = (o * gn_ref[...] * ga_ref[rsl, :]).astype(oa_ref.dtype)


def _even_mixer(segs, states, prev, W, j, *, n_batch, seq_len, nseq, t_len, n_sub, fresh, act_dtype, name):
    s0, h0 = states
    n_heads, dk = s0.shape[2], s0.shape[3]
    wid = n_heads * dk
    rows_step = nseq * t_len * n_sub
    n_steps = seq_len // (t_len * n_sub)
    n_alias = len(prev)
    kern = functools.partial(_even_kernel, nseq=nseq, t_len=t_len, n_sub=n_sub, fresh=fresh, n_alias=n_alias)
    col = pl.BlockSpec((rows_step, dk), lambda b, h, s: (b * n_steps + s, h))
    par = pl.BlockSpec((None, 1, dk), lambda b, h, s: (j, 0, h))
    gate_w = pl.BlockSpec((None, None, dk, dk), lambda b, h, s: (j, h, 0, 0))
    state_specs = [pl.BlockSpec((None, nseq, None, dk, dk), lambda b, h, s: (j, b, h, 0, 0)),
                   pl.BlockSpec((None, nseq, 1, dk), lambda b, h, s: (j, b, 0, h))]
    n_in = 15
    return pl.pallas_call(
        kern,
        grid=(n_batch // nseq, n_heads, n_steps),
        in_specs=[col] * 7 + state_specs + [par, gate_w, par, gate_w, par, par]
                 + [pl.BlockSpec(memory_space=pl.ANY)] * n_alias,
        out_specs=[col, col] + state_specs,
        out_shape=[jax.ShapeDtypeStruct((n_batch * seq_len, wid), act_dtype),
                   jax.ShapeDtypeStruct((n_batch * seq_len, wid), act_dtype),
                   jax.ShapeDtypeStruct(s0.shape, F32),
                   jax.ShapeDtypeStruct(h0.shape, F32)],
        input_output_aliases={n_in + a: 2 + a for a in range(n_alias)},
        compiler_params=_compiler_params(("parallel", "parallel", "arbitrary")),
        name=name,
    )(*segs, s0, h0, W['hgrn_gnorm'], W['lru_wa'], W['lru_ba'], W['lru_wx'], W['lru_bx'], W['lru_lam'], *prev)


def _odd_kernel(zs_ref, xs_ref, bm_ref, cm_ref, dt_ref, s0_ref, alog_ref, dx_ref, gn_ref, ecol_ref, ecols_ref,
                *rest, nseq, t_len, n_sub, n_steps, hpg, n_alias):
    act_ref, s_ref, st_ref = rest[n_alias:]
    carry_t = n_sub * n_steps > 1
    rows = nseq * t_len
    gcols = xs_ref.shape[1]
    p_dim = gcols // hpg
    n_state = bm_ref.shape[1]
    lanes = 2 * p_dim
    group = pl.program_id(1)
    step = pl.program_id(2)

    @pl.when(step == 0)
    def _():
        if carry_t:
            st_ref[...] = jnp.transpose(s0_ref[0])
        else:
            s_ref[...] = s0_ref[...]

    tri = _seq_tri(rows, t_len)
    tri16 = tri.astype(BF16)
    low = lax.broadcasted_iota(jnp.int32, (rows, lanes), 1) < p_dim
    head_lanes = dt_ref.shape[1]
    to_front = lax.rem(head_lanes - group * hpg, head_lanes)
    neg_a = -jnp.exp(alog_ref[...])

    for sub in range(n_sub):
        rsl = slice(sub * rows, (sub + 1) * rows)
        xs = xs_ref[rsl, :]
        b16 = bm_ref[rsl, :].astype(BF16)
        c16 = cm_ref[rsl, :].astype(BF16)
        dt_all = dt_ref[rsl, :]
        dt = pltpu.roll(dt_all, to_front, 1)
        la = pltpu.roll(dt_all * neg_a, to_front, 1)
        cum = _prefix_sum(tri16, la)
        cum_last = _seq_last(cum, nseq, t_len)
        e_cum = jnp.exp(cum)
        w_upd = jnp.exp(cum_last - cum) * dt
        cum_t = jnp.transpose(cum)
        dt_t = jnp.transpose(dt)
        cum_cols = _dot(_split_cat(cum, 3), ecol_ref[...])
        e_cols = _dot(_split_cat(e_cum, 2), ecols_ref[...])
        w_cols = _dot(_split_cat(w_upd, 2), ecols_ref[...])

        cb_g = _dot_nt(c16, b16)
        if carry_t:
            y_int = _dot(c16, st_ref[...].astype(BF16))
        else:
            y_int = []
            for i in range(nseq):
                rs = slice(i * t_len, (i + 1) * t_len)
                y_int.append(_dot_nt(c16[rs, :], s_ref[i].astype(BF16)))
            y_int = y_int[0] if nseq == 1 else jnp.concatenate(y_int, axis=0)
        x_upd = (xs * w_cols).astype(BF16)
        y_cols = []
        for hp in range(hpg // 2):
            cols = slice(2 * hp * p_dim, (2 * hp + 2) * p_dim)
            x_pair = xs[:, cols]
            y_pair = e_cols[:, cols] * y_int[:, cols] + dx_ref[:, cols] * x_pair
            for half in range(2):
                h = 2 * hp + half
                seg = cum_cols[:, h * rows:(h + 1) * rows] - jnp.broadcast_to(cum_t[h:h + 1, :], (rows, rows))
                decay = jnp.exp(jnp.where(tri, seg, -jnp.inf))
                m = (cb_g * (decay * jnp.broadcast_to(dt_t[h:h + 1, :], (rows, rows)))).astype(BF16)
                x_half = jnp.where(low if half == 0 else ~low, x_pair, 0.0).astype(BF16)
                y_pair = y_pair + _dot(m, x_half)
            y_cols.append(y_pair)
        if carry_t:
            st_ref[...] = st_ref[...] * e_cols[rows - 1:rows, :] + _dot_tn(b16, x_upd)
        for i in range(0 if carry_t else nseq):
            rs = slice(i * t_len, (i + 1) * t_len)
            upd = _dot_tn(x_upd[rs, :], b16[rs, :])
            last_t = jnp.transpose(jnp.broadcast_to(cum[(i + 1) * t_len - 1:(i + 1) * t_len, :],
                                                    (head_lanes, head_lanes)))
            for h in range(hpg):
                hr = slice(h * p_dim, (h + 1) * p_dim)
                dec = jnp.exp(jnp.broadcast_to(last_t[h:h + 1, 0:n_state], (p_dim, n_state)))
                s_ref[i, hr, :] = dec * s_ref[i, hr, :] + upd[hr, :]

        y = jnp.concatenate(y_cols, axis=1) * zs_ref[rsl, :]
        y = y * lax.rsqrt(jnp.mean(y * y, axis=-1, keepdims=True) + EPS)
        act_ref[rsl, :] = (y * gn_ref[...]).astype(act_ref.dtype)

    if carry_t:
        @pl.when(step == n_steps - 1)
        def _():
            s_ref[0] = jnp.transpose(st_ref[...])


def _odd_mixer(segs, s0, prev, W, j, *, n_batch, seq_len, nseq, t_len, n_sub, n_heads, act_dtype, name):
    zs, xs, bm, cm, dt = segs
    d_inner, n_state = s0.shape[2], s0.shape[3]
    n_groups = bm.shape[1] // n_state
    hpg = n_heads // n_groups
    gcols = d_inner // n_groups
    head_lanes = dt.shape[1]
    rows = nseq * t_len
    rows_step = rows * n_sub
    n_steps = seq_len // (t_len * n_sub)
    n_alias = len(prev)
    assert nseq == 1 or n_sub * n_steps == 1, "several sequences per step are swept in one sub-chunk"
    kern = functools.partial(_odd_kernel, nseq=nseq, t_len=t_len, n_sub=n_sub, n_steps=n_steps, hpg=hpg,
                             n_alias=n_alias)
    col = lambda width: pl.BlockSpec((rows_step, width), lambda b, g, s: (b * n_steps + s, g))
    par = pl.BlockSpec((None, 1, gcols), lambda b, g, s: (j, 0, g))
    state_spec = pl.BlockSpec((None, nseq, gcols, n_state), lambda b, g, s: (j, b, g, 0))
    head_eye = jnp.eye(head_lanes, hpg, dtype=BF16)
    head_to_rows = jnp.tile(jnp.repeat(head_eye, rows, axis=1), (3, 1))
    head_to_cols = jnp.tile(jnp.repeat(head_eye, gcols // hpg, axis=1), (2, 1))
    n_in = 11
    return pl.pallas_call(
        kern,
        grid=(n_batch // nseq, n_groups, n_steps),
        in_specs=[col(gcols), col(gcols), col(n_state), col(n_state),
                  pl.BlockSpec((rows_step, head_lanes), lambda b, g, s: (b * n_steps + s, 0)), state_spec,
                  _resident((1, head_lanes), j), par, par, _resident(head_to_rows.shape),
                  _resident(head_to_cols.shape)]
                 + [pl.BlockSpec(memory_space=pl.ANY)] * n_alias,
        out_specs=[col(gcols), state_spec],
        out_shape=[jax.ShapeDtypeStruct((n_batch * seq_len, d_inner), act_dtype),
                   jax.ShapeDtypeStruct(s0.shape, F32)],
        scratch_shapes=[pltpu.VMEM((n_state, gcols), F32)],
        input_output_aliases={n_in + a: 1 + a for a in range(n_alias)},
        compiler_params=_compiler_params(("parallel", "parallel", "arbitrary")),
        name=name,
    )(zs, xs, bm, cm, dt, s0, W['ssm_a_log'], W['ssm_dx'], W['ssm_gnorm'], head_to_rows, head_to_cols, *prev)


def _trunk(x, p, even_states, odd_states, fresh, W, cfg, tag):
    n_batch, seq_len, d = x.shape
    m = n_batch * seq_len
    depth = p.shape[0]
    nseq, t_len, n_sub, tm_even, tm_odd, tm_post, act_dtype = cfg
    hgrn0, lru_h0, lru_conv0 = even_states
    ssm0, ssm_conv0 = odd_states
    x = x.reshape(m, d)
    p = p.reshape(depth, m, p.shape[-1])
    mixer_args = dict(n_batch=n_batch, seq_len=seq_len, nseq=nseq, t_len=t_len, n_sub=n_sub, act_dtype=act_dtype)
    even_out, even_conv, odd_out, odd_conv = (), (), (), ()
    for i in range(depth):
        j = i // 2
        if i % 2 == 0:
            *segs, conv = _proj_even(x, lru_conv0, even_conv, W, i, j, seq_len=seq_len, tm=tm_even,
                                     v_dtype=act_dtype, name=f"{tag}_proj{i}")
            even_conv = (conv,)
            act_a, act_b, *even_out = _even_mixer(segs, (hgrn0, lru_h0), even_out, W, j, fresh=fresh,
                                                  name=f"{tag}_even{i}", **mixer_args)
            acts, wo = (act_a, act_b), W['w_even_out']
        else:
            *segs, conv = _proj_odd(x, ssm_conv0, odd_conv, W, i, j, seq_len=seq_len, tm=tm_odd,
                                    bc_dtype=act_dtype, name=f"{tag}_proj{i}")
            odd_conv = (conv,)
            act, *odd_out = _odd_mixer(segs, ssm0, odd_out, W, j, n_heads=W['n_heads_c'],
                                       name=f"{tag}_odd{i}", **mixer_args)
            acts, wo = (act,), W['ssm_out']
        x = _post_mixer(x, acts, p, wo, j, W, i, i == depth - 1, tm_post, f"{tag}_post{i}")
    return x.reshape(n_batch, seq_len, d), even_out, even_conv[0], odd_out[0], odd_conv[0]


def kernel(x_prompt, x_sample, state_hgrn, state_lru_h, state_lru_conv, state_ssm, state_ssm_conv, p_prompt, p_sample, g_mix, g_ffn, g_ple, g_final, w_even_in, hgrn_lb, hgrn_gnorm, lru_conv_w, lru_conv_b, lru_wa, lru_ba, lru_wx, lru_bx, lru_lam, w_even_out, ssm_in, ssm_conv_w, ssm_conv_b, ssm_dt_bias, ssm_a_log, ssm_d, ssm_gnorm, ssm_out, ffn_w1, ffn_w3, ffn_w2, ple_up, ple_gate):
    n_even = state_hgrn.shape[0]
    n_odd, _, n_heads_c, p_c, n_c = state_ssm.shape
    wid = state_lru_h.shape[-1]
    d = x_prompt.shape[-1]
    d_inner = n_heads_c * p_c

    lb = jnp.cumsum(jax.nn.softmax(hgrn_lb.astype(F32), axis=0), axis=0)
    lb = lb - lb[0]
    lbp = jnp.stack([jnp.log(lb), jnp.log1p(-lb), 1.0 - lb], axis=1)
    in_c = ssm_in.shape[-1]
    in_pad = -(-in_c // LANES) * LANES
    head_pad = in_pad - (in_c - n_heads_c)
    pad_h = lambda a: jnp.pad(a.astype(F32), ((0, 0), (0, head_pad - n_heads_c))).reshape(n_odd, 1, head_pad)
    vec = lambda a: a.astype(F32).reshape(a.shape[0], 1, -1)
    W = dict(
        g_mix=vec(g_mix), g_ffn=vec(g_ffn), g_ple=vec(g_ple), g_final=g_final.reshape(1, d),
        lbp=lbp, hgrn_gnorm=vec(hgrn_gnorm), lru_conv_w=lru_conv_w, lru_conv_b=vec(lru_conv_b),
        lru_ba=vec(lru_ba), lru_bx=vec(lru_bx), lru_lam=vec(lru_lam),
        ssm_conv_w=ssm_conv_w, ssm_conv_b=vec(ssm_conv_b), ssm_gnorm=vec(ssm_gnorm),
        ssm_dt_bias=pad_h(ssm_dt_bias), ssm_a_log=pad_h(ssm_a_log),
        ssm_dx=jnp.repeat(ssm_d.astype(F32), p_c, axis=1).reshape(n_odd, 1, d_inner),
        w_even_in=w_even_in.astype(BF16), lru_wa=lru_wa.astype(BF16), lru_wx=lru_wx.astype(BF16),
        w_even_out=w_even_out.astype(BF16),
        ssm_in=jnp.pad(ssm_in.astype(BF16), ((0, 0), (0, 0), (0, in_pad - in_c))),
        ssm_out=ssm_out.astype(BF16), ffn_w1=ffn_w1.astype(BF16), ffn_w3=ffn_w3.astype(BF16),
        ffn_w2=ffn_w2.astype(BF16), ple_up=ple_up.astype(BF16), ple_gate=ple_gate.astype(BF16),
        n_heads_c=n_heads_c)

    def run(x, p, hgrn, lru_h, lru_conv, ssm, ssm_conv, fresh, cfg, tag):
        nb = x.shape[0]
        even_states = (hgrn, lru_h.reshape(n_even, nb, 1, wid), lru_conv)
        odd_states = (ssm.reshape(n_odd, nb, d_inner, n_c), ssm_conv)
        y, (hg, lh), lc, ss, sc = _trunk(x, p, even_states, odd_states, fresh, W, cfg, tag)
        return y, hg, lh.reshape(n_even, nb, wid), lc, ss.reshape(n_odd, nb, n_heads_c, p_c, n_c), sc

    bp = x_prompt.shape[0]
    zeros = lambda ref: jnp.zeros((ref.shape[0], bp) + ref.shape[2:], F32)
    cfg_prompt = (1, 128, 16, 512, 256, 512, BF16)
    cfg_sample = (16, x_sample.shape[1], 1, 512, 256, 256, F32)
    y_p, hg_p, lh_p, lc_p, ss_p, sc_p = run(
        x_prompt, p_prompt, zeros(state_hgrn), zeros(state_lru_h), zeros(state_lru_conv), zeros(state_ssm),
        zeros(state_ssm_conv), True, cfg_prompt, "prompt")
    y_s, hg_s, lh_s, lc_s, ss_s, sc_s = run(
        x_sample, p_sample, state_hgrn, state_lru_h, state_lru_conv, state_ssm, state_ssm_conv, False,
        cfg_sample, "sample")
    return (y_p, y_s, hg_p, hg_s, lh_p, lh_s, lc_p, lc_s, ss_p, ss_s, sc_p, sc_s)
```

```python
import functools
import math

import jax
import jax.numpy as jnp
from jax import lax
from jax.experimental import pallas as pl
from jax.experimental.pallas import tpu as pltpu

F32 = jnp.float32
BF16 = jnp.bfloat16
EPS = 1e-6
LRU_C = 8.0
CONV_W = 4
HIST = CONV_W - 1
SUBLANES = 8
LANES = 128
VMEM_LIMIT = 56 * 1024 * 1024
POST_ROWS = 256


def _dot(a, b):
    return jnp.dot(a, b, preferred_element_type=F32)


def _dot_tn(a, b):
    return lax.dot_general(a, b, (((0,), (0,)), ((), ())), preferred_element_type=F32)


def _dot_nt(a, b):
    return lax.dot_general(a, b, (((1,), (1,)), ((), ())), preferred_element_type=F32)


def _split_cat(x):
    hi, lo = _split_rows(x)
    return jnp.concatenate([hi, lo], axis=1)


def _prefix_sum(tri16, x):
    return _dot(jnp.concatenate([tri16, tri16], axis=1), jnp.concatenate(_split_rows(x), axis=0))


def _split_rows(x):
    hi = x.astype(BF16)
    lo = (x - hi.astype(F32)).astype(BF16)
    return [hi, lo]


def _rms(x, g):
    return x * lax.rsqrt(jnp.mean(x * x, axis=-1, keepdims=True) + EPS) * g


def _sigmoid(x):
    return 0.5 * jnp.tanh(0.5 * x) + 0.5


def _silu(x):
    half = 0.5 * x
    return half * jnp.tanh(half) + half


def _softplus(x):
    return jnp.maximum(x, 0.0) + jnp.log1p(jnp.exp(-jnp.abs(x)))


def _log_sigmoid(x):
    return jnp.minimum(x, 0.0) - jnp.log(1.0 + jnp.exp(-jnp.abs(x)))


def _logaddexp(a, b):
    return jnp.maximum(a, b) + jnp.log(1.0 + jnp.exp(-jnp.abs(a - b)))


def _gelu_tanh(x):
    return 0.5 * x * (1.0 + jnp.tanh(math.sqrt(2.0 / math.pi) * (x + 0.044715 * (x * x * x))))


def _seq_tri(rows, t_len):
    shift = t_len.bit_length() - 1
    r = lax.broadcasted_iota(jnp.int32, (rows, rows), 0)
    c = lax.broadcasted_iota(jnp.int32, (rows, rows), 1)
    return ((r >> shift) == (c >> shift)) & (c <= r)


def _seq_last(x, nseq, t_len):
    x3 = x.reshape(nseq, t_len, x.shape[-1])
    last = x3[:, t_len - 1:t_len, :]
    return jnp.broadcast_to(last, x3.shape).reshape(x.shape)


def _init_hist(hist_ref, c0_ref):
    nseq, _, ch = hist_ref.shape
    hist_ref[:, :SUBLANES - HIST, :] = jnp.zeros((nseq, SUBLANES - HIST, ch), F32)
    hist_ref[:, SUBLANES - HIST:, :] = c0_ref[...]


def _conv_step(raw, hist_ref, w, bias, nseq, t_len):
    rows, ch = raw.shape
    raw3 = raw.reshape(nseq, t_len, ch)
    hist = hist_ref[...].reshape(nseq * SUBLANES, ch)
    row = lax.broadcasted_iota(jnp.int32, (nseq, SUBLANES, ch), 1)
    out = raw * w[HIST:HIST + 1, :] + bias
    for j in range(1, CONV_W):
        rolled = pltpu.roll(raw, j, 0).reshape(nseq, t_len, ch)
        hist_j = pltpu.roll(hist, (nseq * SUBLANES + j - SUBLANES) % (nseq * SUBLANES), 0)
        head = jnp.where(row < j, hist_j.reshape(nseq, SUBLANES, ch), rolled[:, 0:SUBLANES, :])
        shifted = head if t_len == SUBLANES else jnp.concatenate([head, rolled[:, SUBLANES:, :]], axis=1)
        out = out + shifted.reshape(rows, ch) * w[HIST - j:HIST - j + 1, :]
    hist_ref[...] = raw3[:, t_len - SUBLANES:, :]
    return out, raw3[:, t_len - HIST:, :]


def _resident(shape, layer=None):
    nd = len(shape)
    if layer is None:
        return pl.BlockSpec(shape, lambda *_: (0,) * nd, pipeline_mode=pl.Buffered(1))
    return pl.BlockSpec((None,) + tuple(shape), lambda *_: (layer,) + (0,) * nd, pipeline_mode=pl.Buffered(1))


def _compiler_params(semantics):
    return pltpu.CompilerParams(dimension_semantics=semantics, vmem_limit_bytes=VMEM_LIMIT)


def _conv_tiling(m, seq_len, tm):
    if tm >= seq_len:
        return tm // seq_len, seq_len, 1
    return 1, tm, seq_len // tm


def _proj_even_kernel(x_ref, g_ref, w_ref, c0_ref, cw_ref, cb_ref, lbp_ref, *rest, nseq, t_len, tiles_per_seq,
                      n_alias):
    q_ref, lf_ref, k_ref, v_ref, ga_ref, yb_ref, u_ref, co_ref, hist_ref = rest[n_alias:]
    wid = q_ref.shape[1]

    @pl.when(pl.program_id(0) % tiles_per_seq == 0)
    def _():
        _init_hist(hist_ref, c0_ref)

    lbp = lbp_ref[...]
    xn = _rms(x_ref[...], g_ref[...]).astype(BF16)
    seg = lambda s: _dot(xn, w_ref[:, s * wid:(s + 1) * wid])
    q_ref[...] = _silu(seg(0))
    fz = seg(1)
    lf_ref[...] = _logaddexp(lbp[0:1, :], lbp[1:2, :] + _log_sigmoid(fz))
    k_ref[...] = lbp[2:3, :] * _sigmoid(-fz)
    v_ref[...] = seg(2).astype(v_ref.dtype)
    ga_ref[...] = _silu(seg(3))
    yb_ref[...] = _gelu_tanh(seg(4))
    u, new_hist = _conv_step(seg(5), hist_ref, cw_ref[...], cb_ref[...], nseq, t_len)
    u_ref[...] = u
    co_ref[...] = new_hist


def _proj_even(x, c0, prev, W, layer, j, *, seq_len, tm, v_dtype, name):
    m, d = x.shape
    wid = c0.shape[-1]
    nseq, t_len, tiles_per_seq = _conv_tiling(m, seq_len, tm)
    row = lambda i: (i, 0)
    state_spec = pl.BlockSpec((None, nseq, HIST, wid), lambda i: (j, i // tiles_per_seq, 0, 0))
    seg = pl.BlockSpec((tm, wid), row)
    n_alias = len(prev)
    kern = functools.partial(_proj_even_kernel, nseq=nseq, t_len=t_len, tiles_per_seq=tiles_per_seq,
                             n_alias=n_alias)
    n_in = 7
    return pl.pallas_call(
        kern,
        grid=(m // tm,),
        in_specs=[pl.BlockSpec((tm, d), row), _resident((1, d), layer), _resident((d, 6 * wid), j), state_spec,
                  _resident((CONV_W, wid), j), _resident((1, wid), j), _resident((3, wid), j)]
                 + [pl.BlockSpec(memory_space=pl.ANY)] * n_alias,
        out_specs=[seg] * 7 + [state_spec],
        out_shape=[jax.ShapeDtypeStruct((m, wid), dt) for dt in (F32, F32, F32, v_dtype, F32, F32, F32)]
                  + [jax.ShapeDtypeStruct(c0.shape, F32)],
        scratch_shapes=[pltpu.VMEM((nseq, SUBLANES, wid), F32)],
        input_output_aliases={n_in + a: 7 + a for a in range(n_alias)},
        compiler_params=_compiler_params(("arbitrary",)),
        name=name,
    )(x, W['g_mix'], W['w_even_in'], c0, W['lru_conv_w'], W['lru_conv_b'], W['lbp'], *prev)


def _proj_odd_kernel(x_ref, g_ref, w_ref, c0_ref, cw_ref, cb_ref, dtb_ref, *rest, nseq, t_len, tiles_per_seq,
                     n_alias):
    zs_ref, xs_ref, b_ref, c_ref, dt_ref, co_ref, hist_ref = rest[n_alias:]
    d_inner = zs_ref.shape[1]
    conv_dim = hist_ref.shape[2]
    bc_w = b_ref.shape[1]

    @pl.when(pl.program_id(0) % tiles_per_seq == 0)
    def _():
        _init_hist(hist_ref, c0_ref)

    xn = _rms(x_ref[...], g_ref[...]).astype(BF16)
    for c0 in range(0, d_inner, bc_w):
        zs_ref[:, c0:c0 + bc_w] = _silu(_dot(xn, w_ref[:, c0:c0 + bc_w]))
    for c0 in range(0, conv_dim, bc_w):
        cols = slice(c0, c0 + bc_w)
        conv, new_hist = _conv_step(_dot(xn, w_ref[:, d_inner + c0:d_inner + c0 + bc_w]), hist_ref.at[:, :, cols],
                                    cw_ref[:, cols], cb_ref[:, cols], nseq, t_len)
        co_ref[:, :, cols] = new_hist
        act = _silu(conv)
        if c0 < d_inner:
            xs_ref[:, cols] = act
        elif c0 == d_inner:
            b_ref[...] = act.astype(b_ref.dtype)
        else:
            c_ref[...] = act.astype(c_ref.dtype)
    dt_ref[...] = _softplus(_dot(xn, w_ref[:, d_inner + conv_dim:]) + dtb_ref[...])


def _proj_odd(x, c0, prev, W, layer, j, *, seq_len, tm, bc_dtype, name):
    m, d = x.shape
    conv_dim = c0.shape[-1]
    n_in_cols = W['ssm_in'].shape[-1]
    head_lanes = W['ssm_dt_bias'].shape[-1]
    d_inner = n_in_cols - conv_dim - head_lanes
    bc_w = (conv_dim - d_inner) // 2
    nseq, t_len, tiles_per_seq = _conv_tiling(m, seq_len, tm)
    row = lambda i: (i, 0)
    state_spec = pl.BlockSpec((None, nseq, HIST, conv_dim), lambda i: (j, i // tiles_per_seq, 0, 0))
    n_alias = len(prev)
    kern = functools.partial(_proj_odd_kernel, nseq=nseq, t_len=t_len, tiles_per_seq=tiles_per_seq,
                             n_alias=n_alias)
    n_in = 7
    return pl.pallas_call(
        kern,
        grid=(m // tm,),
        in_specs=[pl.BlockSpec((tm, d), row), _resident((1, d), layer), _resident((d, n_in_cols), j), state_spec,
                  _resident((CONV_W, conv_dim), j), _resident((1, conv_dim), j), _resident((1, head_lanes), j)]
                 + [pl.BlockSpec(memory_space=pl.ANY)] * n_alias,
        out_specs=[pl.BlockSpec((tm, d_inner), row), pl.BlockSpec((tm, d_inner), row),
                   pl.BlockSpec((tm, bc_w), row), pl.BlockSpec((tm, bc_w), row),
                   pl.BlockSpec((tm, head_lanes), row), state_spec],
        out_shape=[jax.ShapeDtypeStruct((m, d_inner), F32), jax.ShapeDtypeStruct((m, d_inner), F32),
                   jax.ShapeDtypeStruct((m, bc_w), bc_dtype), jax.ShapeDtypeStruct((m, bc_w), bc_dtype),
                   jax.ShapeDtypeStruct((m, head_lanes), F32), jax.ShapeDtypeStruct(c0.shape, F32)],
        scratch_shapes=[pltpu.VMEM((nseq, SUBLANES, conv_dim), F32)],
        input_output_aliases={n_in + a: 5 + a for a in range(n_alias)},
        compiler_params=_compiler_params(("arbitrary",)),
        name=name,
    )(x, W['g_mix'], W['ssm_in'], c0, W['ssm_conv_w'], W['ssm_conv_b'], W['ssm_dt_bias'], *prev)


def _post_kernel(*refs, n_act, final):
    x_ref, p_ref = refs[0], refs[1]
    act_refs = refs[2:2 + n_act]
    wo_refs = refs[2 + n_act:2 + 2 * n_act]
    gf_ref, w1_ref, w3_ref, w2_ref, wg_ref, wu_ref, gp_ref, gfin_ref, o_ref = refs[2 + 2 * n_act:]
    tm = x_ref.shape[0]
    rb = min(tm, POST_ROWS)
    for r0 in range(0, tm, rb):
        rs = slice(r0, r0 + rb)
        x = x_ref[rs, :]
        for act_ref, wo_ref in zip(act_refs, wo_refs):
            x = x + _dot(act_ref[rs, :].astype(BF16), wo_ref[...])
        xn = _rms(x, gf_ref[...]).astype(BF16)
        h = (_silu(_dot(xn, w1_ref[...])) * _dot(xn, w3_ref[...])).astype(BF16)
        x = x + _dot(h, w2_ref[...])
        gate = _sigmoid(_dot(x.astype(BF16), wg_ref[...]))
        emb = _dot(p_ref[rs, :].astype(BF16), wu_ref[...])
        x = x + _rms(gate * emb, gp_ref[...])
        if final:
            x = _rms(x, gfin_ref[...])
        o_ref[rs, :] = x


def _post_mixer(x, acts, p, wo, wo_layer, W, layer, final, tm, name):
    m, d = x.shape
    dp = p.shape[-1]
    dff = W['ffn_w1'].shape[-1]
    row = lambda i: (i, 0)
    act_specs, wo_specs, off = [], [], 0
    for a in acts:
        ka = a.shape[1]
        act_specs.append(pl.BlockSpec((tm, ka), row))
        wo_specs.append(pl.BlockSpec((None, ka, d), lambda i, blk=off // ka: (wo_layer, blk, 0),
                                     pipeline_mode=pl.Buffered(1)))
        off += ka
    return pl.pallas_call(
        functools.partial(_post_kernel, n_act=len(acts), final=final),
        grid=(m // tm,),
        in_specs=[pl.BlockSpec((tm, d), row), pl.BlockSpec((None, tm, dp), lambda i: (layer, i, 0))]
                 + act_specs + wo_specs + [
                  _resident((1, d), layer), _resident((d, dff), layer), _resident((d, dff), layer),
                  _resident((dff, d), layer), _resident((d, d), layer), _resident((dp, d), layer),
                  _resident((1, d), layer), _resident((1, d))],
        out_specs=pl.BlockSpec((tm, d), row),
        out_shape=jax.ShapeDtypeStruct((m, d), F32),
        compiler_params=_compiler_params(("parallel",)),
        name=name,
    )(x, p, *acts, *([wo] * len(acts)), W['g_ffn'], W['ffn_w1'], W['ffn_w3'], W['ffn_w2'], W['ple_gate'],
      W['ple_up'], W['g_ple'], W['g_final'])


def _even_kernel(q_ref, lf_ref, k_ref, v_ref, ga_ref, yb_ref, u_ref, s0_ref, h0_ref, gn_ref, wa_ref, ba_ref,
                 wx_ref, bx_ref, lam_ref, *rest, nseq, t_len, n_sub, fresh, n_alias):
    oa_ref, ob_ref, s_ref, h_ref = rest[n_alias:]
    rows = nseq * t_len
    wid = q_ref.shape[1]
    levels = t_len.bit_length() - 1
    tiles = t_len // SUBLANES
    step = pl.program_id(2)

    @pl.when(step == 0)
    def _():
        s_ref[...] = s0_ref[...]
        h_ref[...] = h0_ref[...]

    t_in = lax.broadcasted_iota(jnp.int32, (rows, wid), 0) & (t_len - 1)
    t_sub = t_in & (SUBLANES - 1)
    tri16 = _seq_tri(rows, t_len).astype(BF16)
    r_i = lax.broadcasted_iota(jnp.int32, (rows, rows), 0)
    c_i = lax.broadcasted_iota(jnp.int32, (rows, rows), 1)
    pair_level = jnp.where(c_i > r_i, -1, 32 - lax.clz(r_i ^ c_i))
    soft_lam = _softplus(-lam_ref[...])

    for sub in range(n_sub):
        rsl = slice(sub * rows, (sub + 1) * rows)
        u = u_ref[rsl, :]
        u16 = u.astype(BF16)
        r = _sigmoid(_dot(u16, wa_ref[...]) + ba_ref[...])
        gi = _sigmoid(_dot(u16, wx_ref[...]) + bx_ref[...])
        log_a = (-LRU_C) * r * soft_lam
        a = jnp.exp(log_a)
        mult = jnp.sqrt(-jnp.tanh(log_a) * (a * a + 1.0))
        if fresh and sub == 0:
            mult = jnp.where((t_in == 0) & (step == 0), 1.0, mult)
        bt = mult * (gi * u)
        shift = 1
        while shift < SUBLANES:
            valid = t_sub >= shift
            a_sh = pltpu.roll(a, shift, 0)
            b_sh = pltpu.roll(bt, shift, 0)
            bt = jnp.where(valid, a * b_sh + bt, bt)
            a = jnp.where(valid, a * a_sh, a)
            shift *= 2
        a4 = a.reshape(nseq, tiles, SUBLANES, wid)
        b4 = bt.reshape(nseq, tiles, SUBLANES, wid)
        carry = h_ref[...]
        h_tiles = []
        for tile in range(tiles):
            h_k = a4[:, tile] * carry + b4[:, tile]
            carry = h_k[:, SUBLANES - 1:SUBLANES, :]
            h_tiles.append(h_k)
        h_ref[...] = carry
        hseq = (h_tiles[0] if tiles == 1 else jnp.concatenate(h_tiles, axis=1)).reshape(rows, wid)
        ob_ref[rsl, :] = (yb_ref[rsl, :] * hseq).astype(ob_ref.dtype)

        logf = lf_ref[rsl, :]
        k = k_ref[rsl, :]
        q = q_ref[rsl, :]
        v16 = v_ref[rsl, :].astype(BF16)
        b = _prefix_sum(tri16, logf)
        b_last = _seq_last(b, nseq, t_len)
        qe16 = (q * jnp.exp(b)).astype(BF16)
        kd16 = (k * jnp.exp(b_last - b)).astype(BF16)
        q16 = q.astype(BF16)
        k16 = k.astype(BF16)

        att = jnp.where(pair_level == 0, _dot_nt(q16, k16), 0.0)
        for lvl in range(1, levels + 1):
            blk = 1 << lvl
            if lvl == 1:
                e = jnp.where((t_in & 1) == 1, logf, 0.0)
            elif lvl == 2:
                nxt = pltpu.roll(logf, rows - 1, 0)
                prv = pltpu.roll(logf, 1, 0)
                pos = t_in & 3
                e = jnp.where(pos == 0, nxt, jnp.where(pos == 1, 0.0, jnp.where(pos == 2, logf, logf + prv)))
            else:
                b3 = b.reshape(rows // blk, blk, wid)
                mid = jnp.broadcast_to(b3[:, blk // 2 - 1:blk // 2, :], b3.shape).reshape(rows, wid)
                e = -jnp.abs(b - mid)
            w16 = jnp.exp(e).astype(BF16)
            att = jnp.where(pair_level == lvl, _dot_nt(q16 * w16, k16 * w16), att)

        o = _dot(att.astype(BF16), v16)
        inter = []
        for i in range(nseq):
            rs = slice(i * t_len, (i + 1) * t_len)
            s_old = s_ref[i]
            inter.append(_dot(qe16[rs, :], s_old.astype(BF16)))
            e_last = jnp.exp(b[(i + 1) * t_len - 1:(i + 1) * t_len, :])
            scale = jnp.transpose(jnp.broadcast_to(e_last, (wid, wid)))
            s_ref[i] = scale * s_old + _dot_tn(kd16[rs, :], v16[rs, :])
        o = o + (inter[0] if nseq == 1 else jnp.concatenate(inter, axis=0))
        o = o * lax.rsqrt(jnp.mean(o * o, axis=-1, keepdims=True) + EPS)
        oa_ref[rsl, :] = (o * gn_ref[...] * ga_ref[rsl, :]).astype(oa_ref.dtype)


def _even_mixer(segs, states, prev, W, j, *, n_batch, seq_len, nseq, t_len, n_sub, fresh, act_dtype, name):
    s0, h0 = states
    n_heads, dk = s0.shape[2], s0.shape[3]
    wid = n_heads * dk
    rows_step = nseq * t_len * n_sub
    n_steps = seq_len // (t_len * n_sub)
    n_alias = len(prev)
    kern = functools.partial(_even_kernel, nseq=nseq, t_len=t_len, n_sub=n_sub, fresh=fresh, n_alias=n_alias)
    col = pl.BlockSpec((rows_step, dk), lambda b, h, s: (b * n_steps + s, h))
    par = pl.BlockSpec((None, 1, dk), lambda b, h, s: (j, 0, h))
    gate_w = pl.BlockSpec((None, None, dk, dk), lambda b, h, s: (j, h, 0, 0))
    state_specs = [pl.BlockSpec((None, nseq, None, dk, dk), lambda b, h, s: (j, b, h, 0, 0)),
                   pl.BlockSpec((None, nseq, 1, dk), lambda b, h, s: (j, b, 0, h))]
    n_in = 15
    return pl.pallas_call(
        kern,
        grid=(n_batch // nseq, n_heads, n_steps),
        in_specs=[col] * 7 + state_specs + [par, gate_w, par, gate_w, par, par]
                 + [pl.BlockSpec(memory_space=pl.ANY)] * n_alias,
        out_specs=[col, col] + state_specs,
        out_shape=[jax.ShapeDtypeStruct((n_batch * seq_len, wid), act_dtype),
                   jax.ShapeDtypeStruct((n_batch * seq_len, wid), act_dtype),
                   jax.ShapeDtypeStruct(s0.shape, F32),
                   jax.ShapeDtypeStruct(h0.shape, F32)],
        input_output_aliases={n_in + a: 2 + a for a in range(n_alias)},
        compiler_params=_compiler_params(("parallel", "parallel", "arbitrary")),
        name=name,
    )(*segs, s0, h0, W['hgrn_gnorm'], W['lru_wa'], W['lru_ba'], W['lru_wx'], W['lru_bx'], W['lru_lam'], *prev)


def _odd_kernel(zs_ref, xs_ref, bm_ref, cm_ref, dt_ref, s0_ref, alog_ref, dx_ref, gn_ref, ecols_ref, *rest,
                nseq, t_len, n_sub, n_steps, hpg, n_alias):
    act_ref, s_ref, st_ref = rest[n_alias:]
    carry_t = n_sub * n_steps > 1
    rows = nseq * t_len
    gcols = xs_ref.shape[1]
    p_dim = gcols // hpg
    n_state = bm_ref.shape[1]
    lanes = 2 * p_dim
    group = pl.program_id(1)
    step = pl.program_id(2)

    @pl.when(step == 0)
    def _():
        if carry_t:
            st_ref[...] = jnp.transpose(s0_ref[0])
        else:
            s_ref[...] = s0_ref[...]

    tri = _seq_tri(rows, t_len)
    tri16 = tri.astype(BF16)
    low = lax.broadcasted_iota(jnp.int32, (rows, lanes), 1) < p_dim
    head_lanes = dt_ref.shape[1]
    to_front = lax.rem(head_lanes - group * hpg, head_lanes)
    neg_a = -jnp.exp(alog_ref[...])

    for sub in range(n_sub):
        rsl = slice(sub * rows, (sub + 1) * rows)
        xs = xs_ref[rsl, :]
        b16 = bm_ref[rsl, :].astype(BF16)
        c16 = cm_ref[rsl, :].astype(BF16)
        dt_all = dt_ref[rsl, :]
        dt = pltpu.roll(dt_all, to_front, 1)
        la = pltpu.roll(dt_all * neg_a, to_front, 1)
        cum = _prefix_sum(tri16, la)
        cum_last = _seq_last(cum, nseq, t_len)
        cum_t = jnp.transpose(cum)
        dt_t = jnp.transpose(dt)
        e_cols = _dot(_split_cat(jnp.exp(cum)), ecols_ref[...])
        w_cols = _dot(_split_cat(jnp.exp(cum_last - cum) * dt), ecols_ref[...])

        cb_g = _dot_nt(c16, b16)
        if carry_t:
            y_int = _dot(c16, st_ref[...].astype(BF16))
        else:
            y_int = []
            for i in range(nseq):
                rs = slice(i * t_len, (i + 1) * t_len)
                y_int.append(_dot_nt(c16[rs, :], s_ref[i].astype(BF16)))
            y_int = y_int[0] if nseq == 1 else jnp.concatenate(y_int, axis=0)
        x_upd = (xs * w_cols).astype(BF16)
        y_cols = []
        for hp in range(hpg // 2):
            cols = slice(2 * hp * p_dim, (2 * hp + 2) * p_dim)
            x_pair = xs[:, cols]
            m_pair, x_blocks = [], []
            for half in range(2):
                h = 2 * hp + half
                cum_row = jnp.broadcast_to(cum_t[h:h + 1, :], (rows, rows))
                seg = jnp.transpose(cum_row) - cum_row
                decay = jnp.exp(jnp.where(tri, seg, -jnp.inf))
                m_pair.append((cb_g * (decay * jnp.broadcast_to(dt_t[h:h + 1, :], (rows, rows)))).astype(BF16))
                x_blocks.append(jnp.where(low if half == 0 else ~low, x_pair, 0.0).astype(BF16))
            y_cols.append(e_cols[:, cols] * y_int[:, cols] + dx_ref[:, cols] * x_pair
                          + _dot(jnp.concatenate(m_pair, axis=1), jnp.concatenate(x_blocks, axis=0)))
        if carry_t:
            st_ref[...] = st_ref[...] * e_cols[rows - 1:rows, :] + _dot_tn(b16, x_upd)
        for i in range(0 if carry_t else nseq):
            rs = slice(i * t_len, (i + 1) * t_len)
            upd = _dot_tn(x_upd[rs, :], b16[rs, :])
            last_t = jnp.transpose(jnp.broadcast_to(cum[(i + 1) * t_len - 1:(i + 1) * t_len, :],
                                                    (head_lanes, head_lanes)))
            for h in range(hpg):
                hr = slice(h * p_dim, (h + 1) * p_dim)
                dec = jnp.exp(jnp.broadcast_to(last_t[h:h + 1, 0:n_state], (p_dim, n_state)))
                s_ref[i, hr, :] = dec * s_ref[i, hr, :] + upd[hr, :]

        y = jnp.concatenate(y_cols, axis=1) * zs_ref[rsl, :]
        y = y * lax.rsqrt(jnp.mean(y * y, axis=-1, keepdims=True) + EPS)
        act_ref[rsl, :] = (y * gn_ref[...]).astype(act_ref.dtype)

    if carry_t:
        @pl.when(step == n_steps - 1)
        def _():
            s_ref[0] = jnp.transpose(st_ref[...])


def _odd_mixer(segs, s0, prev, W, j, *, n_batch, seq_len, nseq, t_len, n_sub, n_heads, act_dtype, name):
    zs, xs, bm, cm, dt = segs
    d_inner, n_state = s0.shape[2], s0.shape[3]
    n_groups = bm.shape[1] // n_state
    hpg = n_heads // n_groups
    gcols = d_inner // n_groups
    head_lanes = dt.shape[1]
    rows = nseq * t_len
    rows_step = rows * n_sub
    n_steps = seq_len // (t_len * n_sub)
    n_alias = len(prev)
    assert nseq == 1 or n_sub * n_steps == 1, "several sequences per step are swept in one sub-chunk"
    kern = functools.partial(_odd_kernel, nseq=nseq, t_len=t_len, n_sub=n_sub, n_steps=n_steps, hpg=hpg,
                             n_alias=n_alias)
    col = lambda width: pl.BlockSpec((rows_step, width), lambda b, g, s: (b * n_steps + s, g))
    par = pl.BlockSpec((None, 1, gcols), lambda b, g, s: (j, 0, g))
    state_spec = pl.BlockSpec((None, nseq, gcols, n_state), lambda b, g, s: (j, b, g, 0))
    head_to_cols = jnp.tile(jnp.repeat(jnp.eye(head_lanes, hpg, dtype=BF16), gcols // hpg, axis=1), (2, 1))
    n_in = 10
    return pl.pallas_call(
        kern,
        grid=(n_batch // nseq, n_groups, n_steps),
        in_specs=[col(gcols), col(gcols), col(n_state), col(n_state),
                  pl.BlockSpec((rows_step, head_lanes), lambda b, g, s: (b * n_steps + s, 0)), state_spec,
                  _resident((1, head_lanes), j), par, par, _resident(head_to_cols.shape)]
                 + [pl.BlockSpec(memory_space=pl.ANY)] * n_alias,
        out_specs=[col(gcols), state_spec],
        out_shape=[jax.ShapeDtypeStruct((n_batch * seq_len, d_inner), act_dtype),
                   jax.ShapeDtypeStruct(s0.shape, F32)],
        scratch_shapes=[pltpu.VMEM((n_state, gcols), F32)],
        input_output_aliases={n_in + a: 1 + a for a in range(n_alias)},
        compiler_params=_compiler_params(("parallel", "parallel", "arbitrary")),
        name=name,
    )(zs, xs, bm, cm, dt, s0, W['ssm_a_log'], W['ssm_dx'], W['ssm_gnorm'], head_to_cols, *prev)


def _trunk(x, p, even_states, odd_states, fresh, W, cfg, tag):
    n_batch, seq_len, d = x.shape
    m = n_batch * seq_len
    depth = p.shape[0]
    nseq, t_len, n_sub, tm_even, tm_odd, tm_post, act_dtype = cfg
    hgrn0, lru_h0, lru_conv0 = even_states
    ssm0, ssm_conv0 = odd_states
    x = x.reshape(m, d)
    p = p.reshape(depth, m, p.shape[-1])
    mixer_args = dict(n_batch=n_batch, seq_len=seq_len, nseq=nseq, t_len=t_len, n_sub=n_sub, act_dtype=act_dtype)
    even_out, even_conv, odd_out, odd_conv = (), (), (), ()
    for i in range(depth):
        j = i // 2
        if i % 2 == 0:
            *segs, conv = _proj_even(x, lru_conv0, even_conv, W, i, j, seq_len=seq_len, tm=tm_even,
                                     v_dtype=act_dtype, name=f"{tag}_proj{i}")
            even_conv = (conv,)
            act_a, act_b, *even_out = _even_mixer(segs, (hgrn0, lru_h0), even_out, W, j, fresh=fresh,
                                                  name=f"{tag}_even{i}", **mixer_args)
            acts, wo = (act_a, act_b), W['w_even_out']
        else:
            *segs, conv = _proj_odd(x, ssm_conv0, odd_conv, W, i, j, seq_len=seq_len, tm=tm_odd,
                                    bc_dtype=act_dtype, name=f"{tag}_proj{i}")
            odd_conv = (conv,)
            act, *odd_out = _odd_mixer(segs, ssm0, odd_out, W, j, n_heads=W['n_heads_c'],
                                       name=f"{tag}_odd{i}", **mixer_args)
            acts, wo = (act,), W['ssm_out']
        x = _post_mixer(x, acts, p, wo, j, W, i, i == depth - 1, tm_post, f"{tag}_post{i}")
    return x.reshape(n_batch, seq_len, d), even_out, even_conv[0], odd_out[0], odd_conv[0]


def kernel(x_prompt, x_sample, state_hgrn, state_lru_h, state_lru_conv, state_ssm, state_ssm_conv, p_prompt, p_sample, g_mix, g_ffn, g_ple, g_final, w_even_in, hgrn_lb, hgrn_gnorm, lru_conv_w, lru_conv_b, lru_wa, lru_ba, lru_wx, lru_bx, lru_lam, w_even_out, ssm_in, ssm_conv_w, ssm_conv_b, ssm_dt_bias, ssm_a_log, ssm_d, ssm_gnorm, ssm_out, ffn_w1, ffn_w3, ffn_w2, ple_up, ple_gate):
    n_even = state_hgrn.shape[0]
    n_odd, _, n_heads_c, p_c, n_c = state_ssm.shape
    wid = state_lru_h.shape[-1]
    d = x_prompt.shape[-1]
    d_inner = n_heads_c * p_c

    lb = jnp.cumsum(jax.nn.softmax(hgrn_lb.astype(F32), axis=0), axis=0)
    lb = lb - lb[0]
    lbp = jnp.stack([jnp.log(lb), jnp.log1p(-lb), 1.0 - lb], axis=1)
    in_c = ssm_in.shape[-1]
    in_pad = -(-in_c // LANES) * LANES
    head_pad = in_pad - (in_c - n_heads_c)
    pad_h = lambda a: jnp.pad(a.astype(F32), ((0, 0), (0, head_pad - n_heads_c))).reshape(n_odd, 1, head_pad)
    vec = lambda a: a.astype(F32).reshape(a.shape[0], 1, -1)
    W = dict(
        g_mix=vec(g_mix), g_ffn=vec(g_ffn), g_ple=vec(g_ple), g_final=g_final.reshape(1, d),
        lbp=lbp, hgrn_gnorm=vec(hgrn_gnorm), lru_conv_w=lru_conv_w, lru_conv_b=vec(lru_conv_b),
        lru_ba=vec(lru_ba), lru_bx=vec(lru_bx), lru_lam=vec(lru_lam),
        ssm_conv_w=ssm_conv_w, ssm_conv_b=vec(ssm_conv_b), ssm_gnorm=vec(ssm_gnorm),
        ssm_dt_bias=pad_h(ssm_dt_bias), ssm_a_log=pad_h(ssm_a_log),
        ssm_dx=jnp.repeat(ssm_d.astype(F32), p_c, axis=1).reshape(n_odd, 1, d_inner),
        w_even_in=w_even_in.astype(BF16), lru_wa=lru_wa.astype(BF16), lru_wx=lru_wx.astype(BF16),
        w_even_out=w_even_out.astype(BF16),
        ssm_in=jnp.pad(ssm_in.astype(BF16), ((0, 0), (0, 0), (0, in_pad - in_c))),
        ssm_out=ssm_out.astype(BF16), ffn_w1=ffn_w1.astype(BF16), ffn_w3=ffn_w3.astype(BF16),
        ffn_w2=ffn_w2.astype(BF16), ple_up=ple_up.astype(BF16), ple_gate=ple_gate.astype(BF16),
        n_heads_c=n_heads_c)

    def run(x, p, hgrn, lru_h, lru_conv, ssm, ssm_conv, fresh, cfg, tag):
        nb = x.shape[0]
        even_states = (hgrn, lru_h.reshape(n_even, nb, 1, wid), lru_conv)
        odd_states = (ssm.reshape(n_odd, nb, d_inner, n_c), ssm_conv)
        y, (hg, lh), lc, ss, sc = _trunk(x, p, even_states, odd_states, fresh, W, cfg, tag)
        return y, hg, lh.reshape(n_even, nb, wid), lc, ss.reshape(n_odd, nb, n_heads_c, p_c, n_c), sc

    bp = x_prompt.shape[0]
    zeros = lambda ref: jnp.zeros((ref.shape[0], bp) + ref.shape[2:], F32)
    cfg_prompt = (1, 128, 16, 512, 512, 512, BF16)
    cfg_sample = (16, x_sample.shape[1], 1, 512, 512, 256, F32)
    y_p, hg_p, lh_p, lc_p, ss_p, sc_p = run(
        x_prompt, p_prompt, zeros(state_hgrn), zeros(state_lru_h), zeros(state_lru_conv), zeros(state_ssm),
        zeros(state_ssm_conv), True, cfg_prompt, "prompt")
    y_s, hg_s, lh_s, lc_s, ss_s, sc_s = run(
        x_sample, p_sample, state_hgrn, state_lru_h, state_lru_conv, state_ssm, state_ssm_conv, False,
        cfg_sample, "sample")
    return (y_p, y_s, hg_p, hg_s, lh_p, lh_s, lc_p, lc_s, ss_p, ss_s, sc_p, sc_s)
```

```python
import functools
import math

import jax
import jax.numpy as jnp
from jax import lax
from jax.experimental import pallas as pl
from jax.experimental.pallas import tpu as pltpu

F32 = jnp.float32
BF16 = jnp.bfloat16
EPS = 1e-6
LRU_C = 8.0
CONV_W = 4
HIST = CONV_W - 1
SUBLANES = 8
LANES = 128
VMEM_LIMIT = 56 * 1024 * 1024
POST_ROWS = 256


def _dot(a, b):
    return jnp.dot(a, b, preferred_element_type=F32)


def _dot_tn(a, b):
    return lax.dot_general(a, b, (((0,), (0,)), ((), ())), preferred_element_type=F32)


def _dot_nt(a, b):
    return lax.dot_general(a, b, (((1,), (1,)), ((), ())), preferred_element_type=F32)


def _split_cat(x):
    hi, lo = _split_rows(x)
    return jnp.concatenate([hi, lo], axis=1)


def _prefix_sum(tri16, x):
    return _dot(jnp.concatenate([tri16, tri16], axis=1), jnp.concatenate(_split_rows(x), axis=0))


def _split_rows(x):
    hi = x.astype(BF16)
    lo = (x - hi.astype(F32)).astype(BF16)
    return [hi, lo]


def _rms(x, g):
    return x * lax.rsqrt(jnp.mean(x * x, axis=-1, keepdims=True) + EPS) * g


def _sigmoid(x):
    return 0.5 * jnp.tanh(0.5 * x) + 0.5


def _silu(x):
    half = 0.5 * x
    return half * jnp.tanh(half) + half


def _softplus(x):
    return jnp.maximum(x, 0.0) + jnp.log1p(jnp.exp(-jnp.abs(x)))


def _log_sigmoid(x):
    return jnp.minimum(x, 0.0) - jnp.log(1.0 + jnp.exp(-jnp.abs(x)))


def _logaddexp(a, b):
    return jnp.maximum(a, b) + jnp.log(1.0 + jnp.exp(-jnp.abs(a - b)))


def _gelu_tanh(x):
    return 0.5 * x * (1.0 + jnp.tanh(math.sqrt(2.0 / math.pi) * (x + 0.044715 * (x * x * x))))


def _seq_tri(rows, t_len):
    shift = t_len.bit_length() - 1
    r = lax.broadcasted_iota(jnp.int32, (rows, rows), 0)
    c = lax.broadcasted_iota(jnp.int32, (rows, rows), 1)
    return ((r >> shift) == (c >> shift)) & (c <= r)


def _seq_last(x, nseq, t_len):
    x3 = x.reshape(nseq, t_len, x.shape[-1])
    last = x3[:, t_len - 1:t_len, :]
    return jnp.broadcast_to(last, x3.shape).reshape(x.shape)


def _init_hist(hist_ref, c0_ref):
    nseq, _, ch = hist_ref.shape
    hist_ref[:, :SUBLANES - HIST, :] = jnp.zeros((nseq, SUBLANES - HIST, ch), F32)
    hist_ref[:, SUBLANES - HIST:, :] = c0_ref[...]


def _shift_rows(cur, hist, j, nseq, t_len):
    rows, ch = cur.shape
    row = lax.broadcasted_iota(jnp.int32, (nseq, SUBLANES, ch), 1)
    rolled = pltpu.roll(cur, j, 0).reshape(nseq, t_len, ch)
    hist_j = pltpu.roll(hist, (nseq * SUBLANES + j - SUBLANES) % (nseq * SUBLANES), 0)
    head = jnp.where(row < j, hist_j.reshape(nseq, SUBLANES, ch), rolled[:, 0:SUBLANES, :])
    shifted = head if t_len == SUBLANES else jnp.concatenate([head, rolled[:, SUBLANES:, :]], axis=1)
    return shifted.reshape(rows, ch)


def _conv_step(raw, hist_ref, w, bias, nseq, t_len):
    assert CONV_W == 4
    raw3 = raw.reshape(nseq, t_len, raw.shape[1])
    hist = hist_ref[...].reshape(nseq * SUBLANES, raw.shape[1])
    w0, w1, w2, w3 = (w[k:k + 1, :] for k in range(CONV_W))
    prev = _shift_rows(raw, hist, 1, nseq, t_len)
    pair = raw * w1 + prev * w0
    pair_hist = hist * w1 + pltpu.roll(hist, 1, 0) * w0
    out = raw * w3 + bias + prev * w2 + _shift_rows(pair, pair_hist, 2, nseq, t_len)
    hist_ref[...] = raw3[:, t_len - SUBLANES:, :]
    return out, raw3[:, t_len - HIST:, :]


def _resident(shape, layer=None):
    nd = len(shape)
    if layer is None:
        return pl.BlockSpec(shape, lambda *_: (0,) * nd, pipeline_mode=pl.Buffered(1))
    return pl.BlockSpec((None,) + tuple(shape), lambda *_: (layer,) + (0,) * nd, pipeline_mode=pl.Buffered(1))


def _compiler_params(semantics):
    return pltpu.CompilerParams(dimension_semantics=semantics, vmem_limit_bytes=VMEM_LIMIT)


def _conv_tiling(m, seq_len, tm):
    if tm >= seq_len:
        return tm // seq_len, seq_len, 1
    return 1, tm, seq_len // tm


def _proj_even_kernel(x_ref, g_ref, w_ref, c0_ref, cw_ref, cb_ref, lbp_ref, *rest, nseq, t_len, tiles_per_seq,
                      n_alias):
    q_ref, lf_ref, k_ref, v_ref, ga_ref, yb_ref, u_ref, co_ref, hist_ref = rest[n_alias:]
    wid = q_ref.shape[1]

    @pl.when(pl.program_id(0) % tiles_per_seq == 0)
    def _():
        _init_hist(hist_ref, c0_ref)

    lbp = lbp_ref[...]
    xn = _rms(x_ref[...], g_ref[...]).astype(BF16)
    seg = lambda s: _dot(xn, w_ref[:, s * wid:(s + 1) * wid])
    q_ref[...] = _silu(seg(0))
    fz = seg(1)
    lf_ref[...] = _logaddexp(lbp[0:1, :], lbp[1:2, :] + _log_sigmoid(fz))
    k_ref[...] = lbp[2:3, :] * _sigmoid(-fz)
    v_ref[...] = seg(2).astype(v_ref.dtype)
    ga_ref[...] = _silu(seg(3))
    yb_ref[...] = _gelu_tanh(seg(4))
    u, new_hist = _conv_step(seg(5), hist_ref, cw_ref[...], cb_ref[...], nseq, t_len)
    u_ref[...] = u
    co_ref[...] = new_hist


def _proj_even(x, c0, prev, W, layer, j, *, seq_len, tm, v_dtype, name):
    m, d = x.shape
    wid = c0.shape[-1]
    nseq, t_len, tiles_per_seq = _conv_tiling(m, seq_len, tm)
    row = lambda i: (i, 0)
    state_spec = pl.BlockSpec((None, nseq, HIST, wid), lambda i: (j, i // tiles_per_seq, 0, 0))
    seg = pl.BlockSpec((tm, wid), row)
    n_alias = len(prev)
    kern = functools.partial(_proj_even_kernel, nseq=nseq, t_len=t_len, tiles_per_seq=tiles_per_seq,
                             n_alias=n_alias)
    n_in = 7
    return pl.pallas_call(
        kern,
        grid=(m // tm,),
        in_specs=[pl.BlockSpec((tm, d), row), _resident((1, d), layer), _resident((d, 6 * wid), j), state_spec,
                  _resident((CONV_W, wid), j), _resident((1, wid), j), _resident((3, wid), j)]
                 + [pl.BlockSpec(memory_space=pl.ANY)] * n_alias,
        out_specs=[seg] * 7 + [state_spec],
        out_shape=[jax.ShapeDtypeStruct((m, wid), dt) for dt in (F32, F32, F32, v_dtype, F32, F32, F32)]
                  + [jax.ShapeDtypeStruct(c0.shape, F32)],
        scratch_shapes=[pltpu.VMEM((nseq, SUBLANES, wid), F32)],
        input_output_aliases={n_in + a: 7 + a for a in range(n_alias)},
        compiler_params=_compiler_params(("arbitrary",)),
        name=name,
    )(x, W['g_mix'], W['w_even_in'], c0, W['lru_conv_w'], W['lru_conv_b'], W['lbp'], *prev)


def _proj_odd_kernel(x_ref, g_ref, w_ref, wdt_ref, c0_ref, cw_ref, cb_ref, dtb_ref, *rest, nseq, t_len, tiles_per_seq,
                     n_alias):
    zs_ref, xs_ref, b_ref, c_ref, dt_ref, co_ref, hist_ref = rest[n_alias:]
    d_inner = zs_ref.shape[1]
    conv_dim = hist_ref.shape[2]
    bc_w = b_ref.shape[1]

    @pl.when(pl.program_id(0) % tiles_per_seq == 0)
    def _():
        _init_hist(hist_ref, c0_ref)

    xn = _rms(x_ref[...], g_ref[...]).astype(BF16)
    for c0 in range(0, d_inner, bc_w):
        zs_ref[:, c0:c0 + bc_w] = _silu(_dot(xn, w_ref[:, c0:c0 + bc_w]))
    for c0 in range(0, conv_dim, bc_w):
        cols = slice(c0, c0 + bc_w)
        conv, new_hist = _conv_step(_dot(xn, w_ref[:, d_inner + c0:d_inner + c0 + bc_w]), hist_ref.at[:, :, cols],
                                    cw_ref[:, cols], cb_ref[:, cols], nseq, t_len)
        co_ref[:, :, cols] = new_hist
        act = _silu(conv)
        if c0 < d_inner:
            xs_ref[:, cols] = act
        elif c0 == d_inner:
            b_ref[...] = act.astype(b_ref.dtype)
        else:
            c_ref[...] = act.astype(c_ref.dtype)
    dt_ref[...] = _softplus(_dot(xn, wdt_ref[...]) + dtb_ref[...])


def _proj_odd(x, c0, prev, W, layer, j, *, seq_len, tm, bc_dtype, name):
    m, d = x.shape
    conv_dim = c0.shape[-1]
    n_main = W['ssm_in'].shape[-1]
    head_lanes = W['ssm_dt_bias'].shape[-1]
    d_inner = n_main - conv_dim
    bc_w = (conv_dim - d_inner) // 2
    nseq, t_len, tiles_per_seq = _conv_tiling(m, seq_len, tm)
    row = lambda i: (i, 0)
    state_spec = pl.BlockSpec((None, nseq, HIST, conv_dim), lambda i: (j, i // tiles_per_seq, 0, 0))
    n_alias = len(prev)
    kern = functools.partial(_proj_odd_kernel, nseq=nseq, t_len=t_len, tiles_per_seq=tiles_per_seq,
                             n_alias=n_alias)
    n_in = 8
    return pl.pallas_call(
        kern,
        grid=(m // tm,),
        in_specs=[pl.BlockSpec((tm, d), row), _resident((1, d), layer), _resident((d, n_main), j),
                  _resident((d, head_lanes), j), state_spec,
                  _resident((CONV_W, conv_dim), j), _resident((1, conv_dim), j), _resident((1, head_lanes), j)]
                 + [pl.BlockSpec(memory_space=pl.ANY)] * n_alias,
        out_specs=[pl.BlockSpec((tm, d_inner), row), pl.BlockSpec((tm, d_inner), row),
                   pl.BlockSpec((tm, bc_w), row), pl.BlockSpec((tm, bc_w), row),
                   pl.BlockSpec((tm, head_lanes), row), state_spec],
        out_shape=[jax.ShapeDtypeStruct((m, d_inner), F32), jax.ShapeDtypeStruct((m, d_inner), F32),
                   jax.ShapeDtypeStruct((m, bc_w), bc_dtype), jax.ShapeDtypeStruct((m, bc_w), bc_dtype),
                   jax.ShapeDtypeStruct((m, head_lanes), F32), jax.ShapeDtypeStruct(c0.shape, F32)],
        scratch_shapes=[pltpu.VMEM((nseq, SUBLANES, conv_dim), F32)],
        input_output_aliases={n_in + a: 5 + a for a in range(n_alias)},
        compiler_params=_compiler_params(("arbitrary",)),
        name=name,
    )(x, W['g_mix'], W['ssm_in'], W['ssm_in_dt'], c0, W['ssm_conv_w'], W['ssm_conv_b'], W['ssm_dt_bias'], *prev)


def _post_kernel(*refs, n_act, final):
    x_ref, p_ref = refs[0], refs[1]
    act_refs = refs[2:2 + n_act]
    wo_refs = refs[2 + n_act:2 + 2 * n_act]
    gf_ref, w1_ref, w3_ref, w2_ref, wg_ref, wu_ref, gp_ref, gfin_ref, o_ref = refs[2 + 2 * n_act:]
    tm = x_ref.shape[0]
    rb = min(tm, POST_ROWS)
    for r0 in range(0, tm, rb):
        rs = slice(r0, r0 + rb)
        x = x_ref[rs, :]
        for act_ref, wo_ref in zip(act_refs, wo_refs):
            x = x + _dot(act_ref[rs, :].astype(BF16), wo_ref[...])
        xn = _rms(x, gf_ref[...]).astype(BF16)
        h = (_silu(_dot(xn, w1_ref[...])) * _dot(xn, w3_ref[...])).astype(BF16)
        x = x + _dot(h, w2_ref[...])
        gate = _sigmoid(_dot(x.astype(BF16), wg_ref[...]))
        emb = _dot(p_ref[rs, :].astype(BF16), wu_ref[...])
        x = x + _rms(gate * emb, gp_ref[...])
        if final:
            x = _rms(x, gfin_ref[...])
        o_ref[rs, :] = x


def _post_mixer(x, acts, p, wo, wo_layer, W, layer, final, tm, name):
    m, d = x.shape
    dp = p.shape[-1]
    dff = W['ffn_w1'].shape[-1]
    row = lambda i: (i, 0)
    act_specs, wo_specs, off = [], [], 0
    for a in acts:
        ka = a.shape[1]
        act_specs.append(pl.BlockSpec((tm, ka), row))
        wo_specs.append(pl.BlockSpec((None, ka, d), lambda i, blk=off // ka: (wo_layer, blk, 0),
                                     pipeline_mode=pl.Buffered(1)))
        off += ka
    return pl.pallas_call(
        functools.partial(_post_kernel, n_act=len(acts), final=final),
        grid=(m // tm,),
        in_specs=[pl.BlockSpec((tm, d), row), pl.BlockSpec((None, tm, dp), lambda i: (layer, i, 0))]
                 + act_specs + wo_specs + [
                  _resident((1, d), layer), _resident((d, dff), layer), _resident((d, dff), layer),
                  _resident((dff, d), layer), _resident((d, d), layer), _resident((dp, d), layer),
                  _resident((1, d), layer), _resident((1, d))],
        out_specs=pl.BlockSpec((tm, d), row),
        out_shape=jax.ShapeDtypeStruct((m, d), F32),
        compiler_params=_compiler_params(("parallel",)),
        name=name,
    )(x, p, *acts, *([wo] * len(acts)), W['g_ffn'], W['ffn_w1'], W['ffn_w3'], W['ffn_w2'], W['ple_gate'],
      W['ple_up'], W['g_ple'], W['g_final'])


def _even_kernel(q_ref, lf_ref, k_ref, v_ref, ga_ref, yb_ref, u_ref, s0_ref, h0_ref, gn_ref, wa_ref, ba_ref,
                 wx_ref, bx_ref, lam_ref, *rest, nseq, t_len, n_sub, fresh, n_alias):
    oa_ref, ob_ref, s_ref, h_ref = rest[n_alias:]
    rows = nseq * t_len
    wid = q_ref.shape[1]
    levels = t_len.bit_length() - 1
    tiles = t_len // SUBLANES
    step = pl.program_id(2)

    @pl.when(step == 0)
    def _():
        s_ref[...] = s0_ref[...]
        h_ref[...] = h0_ref[...]

    t_in = lax.broadcasted_iota(jnp.int32, (rows, wid), 0) & (t_len - 1)
    t_sub = t_in & (SUBLANES - 1)
    tri16 = _seq_tri(rows, t_len).astype(BF16)
    r_i = lax.broadcasted_iota(jnp.int32, (rows, rows), 0)
    c_i = lax.broadcasted_iota(jnp.int32, (rows, rows), 1)
    pair_level = jnp.where(c_i > r_i, -1, 32 - lax.clz(r_i ^ c_i))
    soft_lam = _softplus(-lam_ref[...])

    for sub in range(n_sub):
        rsl = slice(sub * rows, (sub + 1) * rows)
        u = u_ref[rsl, :]
        u16 = u.astype(BF16)
        r = _sigmoid(_dot(u16, wa_ref[...]) + ba_ref[...])
        gi = _sigmoid(_dot(u16, wx_ref[...]) + bx_ref[...])
        log_a = (-LRU_C) * r * soft_lam
        a = jnp.exp(log_a)
        mult = jnp.sqrt(-jnp.tanh(log_a) * (a * a + 1.0))
        if fresh and sub == 0:
            mult = jnp.where((t_in == 0) & (step == 0), 1.0, mult)
        bt = mult * (gi * u)
        shift = 1
        while shift < SUBLANES:
            valid = t_sub >= shift
            a_sh = pltpu.roll(a, shift, 0)
            b_sh = pltpu.roll(bt, shift, 0)
            bt = jnp.where(valid, a * b_sh + bt, bt)
            a = jnp.where(valid, a * a_sh, a)
            shift *= 2
        a4 = a.reshape(nseq, tiles, SUBLANES, wid)
        b4 = bt.reshape(nseq, tiles, SUBLANES, wid)
        carry = h_ref[...]
        h_tiles = []
        for tile in range(tiles):
            h_k = a4[:, tile] * carry + b4[:, tile]
            carry = h_k[:, SUBLANES - 1:SUBLANES, :]
            h_tiles.append(h_k)
        h_ref[...] = carry
        hseq = (h_tiles[0] if tiles == 1 else jnp.concatenate(h_tiles, axis=1)).reshape(rows, wid)
        ob_ref[rsl, :] = (yb_ref[rsl, :] * hseq).astype(ob_ref.dtype)

        logf = lf_ref[rsl, :]
        k = k_ref[rsl, :]
        q = q_ref[rsl, :]
        v16 = v_ref[rsl, :].astype(BF16)
        b = _prefix_sum(tri16, logf)
        b_last = _seq_last(b, nseq, t_len)
        qe16 = (q * jnp.exp(b)).astype(BF16)
        kd16 = (k * jnp.exp(b_last - b)).astype(BF16)
        q16 = q.astype(BF16)
        k16 = k.astype(BF16)

        att = jnp.where(pair_level == 0, _dot_nt(q16, k16), 0.0)
        for lvl in range(1, levels + 1):
            blk = 1 << lvl
            if lvl == 1:
                e = jnp.where((t_in & 1) == 1, logf, 0.0)
            elif lvl == 2:
                nxt = pltpu.roll(logf, rows - 1, 0)
                prv = pltpu.roll(logf, 1, 0)
                pos = t_in & 3
                e = jnp.where(pos == 0, nxt, jnp.where(pos == 1, 0.0, jnp.where(pos == 2, logf, logf + prv)))
            else:
                b3 = b.reshape(rows // blk, blk, wid)
                mid = jnp.broadcast_to(b3[:, blk // 2 - 1:blk // 2, :], b3.shape).reshape(rows, wid)
                e = -jnp.abs(b - mid)
            w16 = jnp.exp(e).astype(BF16)
            att = jnp.where(pair_level == lvl, _dot_nt(q16 * w16, k16 * w16), att)

        o = _dot(att.astype(BF16), v16)
        inter = []
        for i in range(nseq):
            rs = slice(i * t_len, (i + 1) * t_len)
            s_old = s_ref[i]
            inter.append(_dot(qe16[rs, :], s_old.astype(BF16)))
            e_last = jnp.exp(b[(i + 1) * t_len - 1:(i + 1) * t_len, :])
            scale = jnp.transpose(jnp.broadcast_to(e_last, (wid, wid)))
            s_ref[i] = scale * s_old + _dot_tn(kd16[rs, :], v16[rs, :])
        o = o + (inter[0] if nseq == 1 else jnp.concatenate(inter, axis=0))
        o = o * lax.rsqrt(jnp.mean(o * o, axis=-1, keepdims=True) + EPS)
        oa_ref[rsl, :] = (o * gn_ref[...] * ga_ref[rsl, :]).astype(oa_ref.dtype)


def _even_mixer(segs, states, prev, W, j, *, n_batch, seq_len, nseq, t_len, n_sub, fresh, act_dtype, name):
    s0, h0 = states
    n_heads, dk = s0.shape[2], s0.shape[3]
    wid = n_heads * dk
    rows_step = nseq * t_len * n_sub
    n_steps = seq_len // (t_len * n_sub)
    n_alias = len(prev)
    kern = functools.partial(_even_kernel, nseq=nseq, t_len=t_len, n_sub=n_sub, fresh=fresh, n_alias=n_alias)
    col = pl.BlockSpec((rows_step, dk), lambda b, h, s: (b * n_steps + s, h))
    par = pl.BlockSpec((None, 1, dk), lambda b, h, s: (j, 0, h))
    gate_w = pl.BlockSpec((None, None, dk, dk), lambda b, h, s: (j, h, 0, 0))
    state_specs = [pl.BlockSpec((None, nseq, None, dk, dk), lambda b, h, s: (j, b, h, 0, 0)),
                   pl.BlockSpec((None, nseq, 1, dk), lambda b, h, s: (j, b, 0, h))]
    n_in = 15
    return pl.pallas_call(
        kern,
        grid=(n_batch // nseq, n_heads, n_steps),
        in_specs=[col] * 7 + state_specs + [par, gate_w, par, gate_w, par, par]
                 + [pl.BlockSpec(memory_space=pl.ANY)] * n_alias,
        out_specs=[col, col] + state_specs,
        out_shape=[jax.ShapeDtypeStruct((n_batch * seq_len, wid), act_dtype),
                   jax.ShapeDtypeStruct((n_batch * seq_len, wid), act_dtype),
                   jax.ShapeDtypeStruct(s0.shape, F32),
                   jax.ShapeDtypeStruct(h0.shape, F32)],
        input_output_aliases={n_in + a: 2 + a for a in range(n_alias)},
        compiler_params=_compiler_params(("parallel", "parallel", "arbitrary")),
        name=name,
    )(*segs, s0, h0, W['hgrn_gnorm'], W['lru_wa'], W['lru_ba'], W['lru_wx'], W['lru_bx'], W['lru_lam'], *prev)


def _odd_kernel(zs_ref, xs_ref, bm_ref, cm_ref, dt_ref, s0_ref, alog_ref, dx_ref, gn_ref, ecols_ref, *rest,
                nseq, t_len, n_sub, n_steps, hpg, n_alias):
    act_ref, s_ref, st_ref = rest[n_alias:]
    carry_t = n_sub * n_steps > 1
    rows = nseq * t_len
    gcols = xs_ref.shape[1]
    p_dim = gcols // hpg
    n_state = bm_ref.shape[1]
    lanes = 2 * p_dim
    group = pl.program_id(1)
    step = pl.program_id(2)

    @pl.when(step == 0)
    def _():
        if carry_t:
            st_ref[...] = jnp.transpose(s0_ref[0])
        else:
            s_ref[...] = s0_ref[...]

    tri = _seq_tri(rows, t_len)
    tri16 = tri.astype(BF16)
    low = lax.broadcasted_iota(jnp.int32, (rows, lanes), 1) < p_dim
    head_lanes = dt_ref.shape[1]
    to_front = lax.rem(head_lanes - group * hpg, head_lanes)
    neg_a = -jnp.exp(alog_ref[...])

    for sub in range(n_sub):
        rsl = slice(sub * rows, (sub + 1) * rows)
        xs = xs_ref[rsl, :]
        b16 = bm_ref[rsl, :].astype(BF16)
        c16 = cm_ref[rsl, :].astype(BF16)
        dt_all = dt_ref[rsl, :]
        dt = pltpu.roll(dt_all, to_front, 1)
        la = pltpu.roll(dt_all * neg_a, to_front, 1)
        cum = _prefix_sum(tri16, la)
        cum_last = _seq_last(cum, nseq, t_len)
        cum_t = jnp.transpose(cum)
        dt_t = jnp.transpose(dt)
        e_cols = _dot(_split_cat(jnp.exp(cum)), ecols_ref[...])
        w_cols = _dot(_split_cat(jnp.exp(cum_last - cum) * dt), ecols_ref[...])

        cb_g = _dot_nt(c16, b16)
        if carry_t:
            y_int = _dot(c16, st_ref[...].astype(BF16))
        else:
            y_int = []
            for i in range(nseq):
                rs = slice(i * t_len, (i + 1) * t_len)
                y_int.append(_dot_nt(c16[rs, :], s_ref[i].astype(BF16)))
            y_int = y_int[0] if nseq == 1 else jnp.concatenate(y_int, axis=0)
        x_upd = (xs * w_cols).astype(BF16)
        y_cols = []
        for hp in range(hpg // 2):
            cols = slice(2 * hp * p_dim, (2 * hp + 2) * p_dim)
            x_pair = xs[:, cols]
            m_pair, x_blocks = [], []
            for half in range(2):
                h = 2 * hp + half
                cum_row = jnp.broadcast_to(cum_t[h:h + 1, :], (rows, rows))
                seg = jnp.transpose(cum_row) - cum_row
                decay = jnp.exp(jnp.where(tri, seg, -jnp.inf))
                m_pair.append((cb_g * (decay * jnp.broadcast_to(dt_t[h:h + 1, :], (rows, rows)))).astype(BF16))
                x_blocks.append(jnp.where(low if half == 0 else ~low, x_pair, 0.0).astype(BF16))
            y_cols.append(e_cols[:, cols] * y_int[:, cols] + dx_ref[:, cols] * x_pair
                          + _dot(jnp.concatenate(m_pair, axis=1), jnp.concatenate(x_blocks, axis=0)))
        if carry_t:
            st_ref[...] = st_ref[...] * e_cols[rows - 1:rows, :] + _dot_tn(b16, x_upd)
        for i in range(0 if carry_t else nseq):
            rs = slice(i * t_len, (i + 1) * t_len)
            upd = _dot_tn(x_upd[rs, :], b16[rs, :])
            last_t = jnp.transpose(jnp.broadcast_to(cum[(i + 1) * t_len - 1:(i + 1) * t_len, :],
                                                    (head_lanes, head_lanes)))
            for h in range(hpg):
                hr = slice(h * p_dim, (h + 1) * p_dim)
                dec = jnp.exp(jnp.broadcast_to(last_t[h:h + 1, 0:n_state], (p_dim, n_state)))
                s_ref[i, hr, :] = dec * s_ref[i, hr, :] + upd[hr, :]

        y = jnp.concatenate(y_cols, axis=1) * zs_ref[rsl, :]
        y = y * lax.rsqrt(jnp.mean(y * y, axis=-1, keepdims=True) + EPS)
        act_ref[rsl, :] = (y * gn_ref[...]).astype(act_ref.dtype)

    if carry_t:
        @pl.when(step == n_steps - 1)
        def _():
            s_ref[0] = jnp.transpose(st_ref[...])


def _odd_mixer(segs, s0, prev, W, j, *, n_batch, seq_len, nseq, t_len, n_sub, n_heads, act_dtype, name):
    zs, xs, bm, cm, dt = segs
    d_inner, n_state = s0.shape[2], s0.shape[3]
    n_groups = bm.shape[1] // n_state
    hpg = n_heads // n_groups
    gcols = d_inner // n_groups
    head_lanes = dt.shape[1]
    rows = nseq * t_len
    rows_step = rows * n_sub
    n_steps = seq_len // (t_len * n_sub)
    n_alias = len(prev)
    assert nseq == 1 or n_sub * n_steps == 1, "several sequences per step are swept in one sub-chunk"
    kern = functools.partial(_odd_kernel, nseq=nseq, t_len=t_len, n_sub=n_sub, n_steps=n_steps, hpg=hpg,
                             n_alias=n_alias)
    col = lambda width: pl.BlockSpec((rows_step, width), lambda b, g, s: (b * n_steps + s, g))
    par = pl.BlockSpec((None, 1, gcols), lambda b, g, s: (j, 0, g))
    state_spec = pl.BlockSpec((None, nseq, gcols, n_state), lambda b, g, s: (j, b, g, 0))
    head_to_cols = jnp.tile(jnp.repeat(jnp.eye(head_lanes, hpg, dtype=BF16), gcols // hpg, axis=1), (2, 1))
    n_in = 10
    return pl.pallas_call(
        kern,
        grid=(n_batch // nseq, n_groups, n_steps),
        in_specs=[col(gcols), col(gcols), col(n_state), col(n_state),
                  pl.BlockSpec((rows_step, head_lanes), lambda b, g, s: (b * n_steps + s, 0)), state_spec,
                  _resident((1, head_lanes), j), par, par, _resident(head_to_cols.shape)]
                 + [pl.BlockSpec(memory_space=pl.ANY)] * n_alias,
        out_specs=[col(gcols), state_spec],
        out_shape=[jax.ShapeDtypeStruct((n_batch * seq_len, d_inner), act_dtype),
                   jax.ShapeDtypeStruct(s0.shape, F32)],
        scratch_shapes=[pltpu.VMEM((n_state, gcols), F32)],
        input_output_aliases={n_in + a: 1 + a for a in range(n_alias)},
        compiler_params=_compiler_params(("parallel", "parallel", "arbitrary")),
        name=name,
    )(zs, xs, bm, cm, dt, s0, W['ssm_a_log'], W['ssm_dx'], W['ssm_gnorm'], head_to_cols, *prev)


def _trunk(x, p, even_states, odd_states, fresh, W, cfg, tag):
    n_batch, seq_len, d = x.shape
    m = n_batch * seq_len
    depth = p.shape[0]
    nseq, t_len, n_sub, tm_even, tm_odd, tm_post, act_dtype = cfg
    hgrn0, lru_h0, lru_conv0 = even_states
    ssm0, ssm_conv0 = odd_states
    x = x.reshape(m, d)
    p = p.reshape(depth, m, p.shape[-1])
    mixer_args = dict(n_batch=n_batch, seq_len=seq_len, nseq=nseq, t_len=t_len, n_sub=n_sub, act_dtype=act_dtype)
    even_out, even_conv, odd_out, odd_conv = (), (), (), ()
    for i in range(depth):
        j = i // 2
        if i % 2 == 0:
            *segs, conv = _proj_even(x, lru_conv0, even_conv, W, i, j, seq_len=seq_len, tm=tm_even,
                                     v_dtype=act_dtype, name=f"{tag}_proj{i}")
            even_conv = (conv,)
            act_a, act_b, *even_out = _even_mixer(segs, (hgrn0, lru_h0), even_out, W, j, fresh=fresh,
                                                  name=f"{tag}_even{i}", **mixer_args)
            acts, wo = (act_a, act_b), W['w_even_out']
        else:
            *segs, conv = _proj_odd(x, ssm_conv0, odd_conv, W, i, j, seq_len=seq_len, tm=tm_odd,
                                    bc_dtype=act_dtype, name=f"{tag}_proj{i}")
            odd_conv = (conv,)
            act, *odd_out = _odd_mixer(segs, ssm0, odd_out, W, j, n_heads=W['n_heads_c'],
                                       name=f"{tag}_odd{i}", **mixer_args)
            acts, wo = (act,), W['ssm_out']
        x = _post_mixer(x, acts, p, wo, j, W, i, i == depth - 1, tm_post, f"{tag}_post{i}")
    return x.reshape(n_batch, seq_len, d), even_out, even_conv[0], odd_out[0], odd_conv[0]


def kernel(x_prompt, x_sample, state_hgrn, state_lru_h, state_lru_conv, state_ssm, state_ssm_conv, p_prompt, p_sample, g_mix, g_ffn, g_ple, g_final, w_even_in, hgrn_lb, hgrn_gnorm, lru_conv_w, lru_conv_b, lru_wa, lru_ba, lru_wx, lru_bx, lru_lam, w_even_out, ssm_in, ssm_conv_w, ssm_conv_b, ssm_dt_bias, ssm_a_log, ssm_d, ssm_gnorm, ssm_out, ffn_w1, ffn_w3, ffn_w2, ple_up, ple_gate):
    n_even = state_hgrn.shape[0]
    n_odd, _, n_heads_c, p_c, n_c = state_ssm.shape
    wid = state_lru_h.shape[-1]
    d = x_prompt.shape[-1]
    d_inner = n_heads_c * p_c

    lb = jnp.cumsum(jax.nn.softmax(hgrn_lb.astype(F32), axis=0), axis=0)
    lb = lb - lb[0]
    lbp = jnp.stack([jnp.log(lb), jnp.log1p(-lb), 1.0 - lb], axis=1)
    n_main = ssm_in.shape[-1] - n_heads_c
    head_pad = -(-n_heads_c // LANES) * LANES
    pad_h = lambda a: jnp.pad(a.astype(F32), ((0, 0), (0, head_pad - n_heads_c))).reshape(n_odd, 1, head_pad)
    vec = lambda a: a.astype(F32).reshape(a.shape[0], 1, -1)
    W = dict(
        g_mix=vec(g_mix), g_ffn=vec(g_ffn), g_ple=vec(g_ple), g_final=g_final.reshape(1, d),
        lbp=lbp, hgrn_gnorm=vec(hgrn_gnorm), lru_conv_w=lru_conv_w, lru_conv_b=vec(lru_conv_b),
        lru_ba=vec(lru_ba), lru_bx=vec(lru_bx), lru_lam=vec(lru_lam),
        ssm_conv_w=ssm_conv_w, ssm_conv_b=vec(ssm_conv_b), ssm_gnorm=vec(ssm_gnorm),
        ssm_dt_bias=pad_h(ssm_dt_bias), ssm_a_log=pad_h(ssm_a_log),
        ssm_dx=jnp.repeat(ssm_d.astype(F32), p_c, axis=1).reshape(n_odd, 1, d_inner),
        w_even_in=w_even_in.astype(BF16), lru_wa=lru_wa.astype(BF16), lru_wx=lru_wx.astype(BF16),
        w_even_out=w_even_out.astype(BF16),
        ssm_in=ssm_in[..., :n_main].astype(BF16),
        ssm_in_dt=jnp.pad(ssm_in[..., n_main:].astype(BF16), ((0, 0), (0, 0), (0, head_pad - n_heads_c))),
        ssm_out=ssm_out.astype(BF16), ffn_w1=ffn_w1.astype(BF16), ffn_w3=ffn_w3.astype(BF16),
        ffn_w2=ffn_w2.astype(BF16), ple_up=ple_up.astype(BF16), ple_gate=ple_gate.astype(BF16),
        n_heads_c=n_heads_c)

    def run(x, p, hgrn, lru_h, lru_conv, ssm, ssm_conv, fresh, cfg, tag):
        nb = x.shape[0]
        even_states = (hgrn, lru_h.reshape(n_even, nb, 1, wid), lru_conv)
        odd_states = (ssm.reshape(n_odd, nb, d_inner, n_c), ssm_conv)
        y, (hg, lh), lc, ss, sc = _trunk(x, p, even_states, odd_states, fresh, W, cfg, tag)
        return y, hg, lh.reshape(n_even, nb, wid), lc, ss.reshape(n_odd, nb, n_heads_c, p_c, n_c), sc

    bp = x_prompt.shape[0]
    zeros = lambda ref: jnp.zeros((ref.shape[0], bp) + ref.shape[2:], F32)
    cfg_prompt = (1, 128, 16, 1024, 512, 512, BF16)
    cfg_sample = (16, x_sample.shape[1], 1, 512, 512, 512, F32)
    y_p, hg_p, lh_p, lc_p, ss_p, sc_p = run(
        x_prompt, p_prompt, zeros(state_hgrn), zeros(state_lru_h), zeros(state_lru_conv), zeros(state_ssm),
        zeros(state_ssm_conv), True, cfg_prompt, "prompt")
    y_s, hg_s, lh_s, lc_s, ss_s, sc_s = run(
        x_sample, p_sample, state_hgrn, state_lru_h, state_lru_conv, state_ssm, state_ssm_conv, False,
        cfg_sample, "sample")
    return (y_p, y_s, hg_p, hg_s, lh_p, lh_s, lc_p, lc_s, ss_p, ss_s, sc_p, sc_s)
```

```python
import functools
import math

import jax
import jax.numpy as jnp
from jax import lax
from jax.experimental import pallas as pl
from jax.experimental.pallas import tpu as pltpu

F32 = jnp.float32
BF16 = jnp.bfloat16
EPS = 1e-6
LRU_C = 8.0
CONV_W = 4
HIST = CONV_W - 1
SUBLANES = 8
LANES = 128
VMEM_LIMIT = 56 * 1024 * 1024
POST_ROWS = 256


def _dot(a, b):
    return jnp.dot(a, b, preferred_element_type=F32)


def _dot_tn(a, b):
    return lax.dot_general(a, b, (((0,), (0,)), ((), ())), preferred_element_type=F32)


def _dot_nt(a, b):
    return lax.dot_general(a, b, (((1,), (1,)), ((), ())), preferred_element_type=F32)


def _split_cat(x):
    hi, lo = _split_rows(x)
    return jnp.concatenate([hi, lo], axis=1)


def _prefix_sum(tri16, x):
    return _dot(jnp.concatenate([tri16, tri16], axis=1), jnp.concatenate(_split_rows(x), axis=0))


def _split_rows(x):
    hi = x.astype(BF16)
    lo = (x - hi.astype(F32)).astype(BF16)
    return [hi, lo]


def _rms(x, g):
    return x * lax.rsqrt(jnp.mean(x * x, axis=-1, keepdims=True) + EPS) * g


def _sigmoid(x):
    return 0.5 * jnp.tanh(0.5 * x) + 0.5


def _silu(x):
    half = 0.5 * x
    return half * jnp.tanh(half) + half


def _softplus(x):
    return jnp.maximum(x, 0.0) + jnp.log1p(jnp.exp(-jnp.abs(x)))


def _log_sigmoid(x):
    return jnp.minimum(x, 0.0) - jnp.log(1.0 + jnp.exp(-jnp.abs(x)))


def _logaddexp(a, b):
    return jnp.maximum(a, b) + jnp.log(1.0 + jnp.exp(-jnp.abs(a - b)))


def _gelu_tanh(x):
    return 0.5 * x * (1.0 + jnp.tanh(math.sqrt(2.0 / math.pi) * (x + 0.044715 * (x * x * x))))


def _seq_tri(rows, t_len):
    shift = t_len.bit_length() - 1
    r = lax.broadcasted_iota(jnp.int32, (rows, rows), 0)
    c = lax.broadcasted_iota(jnp.int32, (rows, rows), 1)
    return ((r >> shift) == (c >> shift)) & (c <= r)


def _seq_last(x, nseq, t_len):
    x3 = x.reshape(nseq, t_len, x.shape[-1])
    last = x3[:, t_len - 1:t_len, :]
    return jnp.broadcast_to(last, x3.shape).reshape(x.shape)


def _init_hist(hist_ref, c0_ref):
    nseq, _, ch = hist_ref.shape
    hist_ref[:, :SUBLANES - HIST, :] = jnp.zeros((nseq, SUBLANES - HIST, ch), F32)
    hist_ref[:, SUBLANES - HIST:, :] = c0_ref[...]


def _shift_rows(cur, hist, j, nseq, t_len):
    rows, ch = cur.shape
    row = lax.broadcasted_iota(jnp.int32, (nseq, SUBLANES, ch), 1)
    rolled = pltpu.roll(cur, j, 0).reshape(nseq, t_len, ch)
    hist_j = pltpu.roll(hist, (nseq * SUBLANES + j - SUBLANES) % (nseq * SUBLANES), 0)
    head = jnp.where(row < j, hist_j.reshape(nseq, SUBLANES, ch), rolled[:, 0:SUBLANES, :])
    shifted = head if t_len == SUBLANES else jnp.concatenate([head, rolled[:, SUBLANES:, :]], axis=1)
    return shifted.reshape(rows, ch)


def _conv_step(raw, hist_ref, w, bias, nseq, t_len):
    assert CONV_W == 4
    raw3 = raw.reshape(nseq, t_len, raw.shape[1])
    hist = hist_ref[...].reshape(nseq * SUBLANES, raw.shape[1])
    w0, w1, w2, w3 = (w[k:k + 1, :] for k in range(CONV_W))
    prev = _shift_rows(raw, hist, 1, nseq, t_len)
    pair = raw * w1 + prev * w0
    pair_hist = hist * w1 + pltpu.roll(hist, 1, 0) * w0
    out = raw * w3 + bias + prev * w2 + _shift_rows(pair, pair_hist, 2, nseq, t_len)
    hist_ref[...] = raw3[:, t_len - SUBLANES:, :]
    return out, raw3[:, t_len - HIST:, :]


def _resident(shape, layer=None):
    nd = len(shape)
    if layer is None:
        return pl.BlockSpec(shape, lambda *_: (0,) * nd, pipeline_mode=pl.Buffered(1))
    return pl.BlockSpec((None,) + tuple(shape), lambda *_: (layer,) + (0,) * nd, pipeline_mode=pl.Buffered(1))


def _compiler_params(semantics):
    return pltpu.CompilerParams(dimension_semantics=semantics, vmem_limit_bytes=VMEM_LIMIT)


def _conv_tiling(m, seq_len, tm):
    if tm >= seq_len:
        return tm // seq_len, seq_len, 1
    return 1, tm, seq_len // tm


def _proj_even_kernel(x_ref, g_ref, w_ref, c0_ref, cw_ref, cb_ref, lbp_ref, *rest, nseq, t_len, tiles_per_seq,
                      n_alias):
    q_ref, lf_ref, k_ref, v_ref, ga_ref, yb_ref, u_ref, co_ref, hist_ref = rest[n_alias:]
    wid = q_ref.shape[1]

    @pl.when(pl.program_id(0) % tiles_per_seq == 0)
    def _():
        _init_hist(hist_ref, c0_ref)

    lbp = lbp_ref[...]
    xn = _rms(x_ref[...], g_ref[...]).astype(BF16)
    seg = lambda s: _dot(xn, w_ref[:, s * wid:(s + 1) * wid])
    q_ref[...] = _silu(seg(0))
    fz = seg(1)
    lf_ref[...] = _logaddexp(lbp[0:1, :], lbp[1:2, :] + _log_sigmoid(fz))
    k_ref[...] = lbp[2:3, :] * _sigmoid(-fz)
    v_ref[...] = seg(2).astype(v_ref.dtype)
    ga_ref[...] = _silu(seg(3))
    yb_ref[...] = _gelu_tanh(seg(4))
    u, new_hist = _conv_step(seg(5), hist_ref, cw_ref[...], cb_ref[...], nseq, t_len)
    u_ref[...] = u
    co_ref[...] = new_hist


def _proj_even(x, c0, prev, W, layer, j, *, seq_len, tm, v_dtype, name):
    m, d = x.shape
    wid = c0.shape[-1]
    nseq, t_len, tiles_per_seq = _conv_tiling(m, seq_len, tm)
    row = lambda i: (i, 0)
    state_spec = pl.BlockSpec((None, nseq, HIST, wid), lambda i: (j, i // tiles_per_seq, 0, 0))
    seg = pl.BlockSpec((tm, wid), row)
    n_alias = len(prev)
    kern = functools.partial(_proj_even_kernel, nseq=nseq, t_len=t_len, tiles_per_seq=tiles_per_seq,
                             n_alias=n_alias)
    n_in = 7
    return pl.pallas_call(
        kern,
        grid=(m // tm,),
        in_specs=[pl.BlockSpec((tm, d), row), _resident((1, d), layer), _resident((d, 6 * wid), j), state_spec,
                  _resident((CONV_W, wid), j), _resident((1, wid), j), _resident((3, wid), j)]
                 + [pl.BlockSpec(memory_space=pl.ANY)] * n_alias,
        out_specs=[seg] * 7 + [state_spec],
        out_shape=[jax.ShapeDtypeStruct((m, wid), dt) for dt in (F32, F32, F32, v_dtype, F32, F32, F32)]
                  + [jax.ShapeDtypeStruct(c0.shape, F32)],
        scratch_shapes=[pltpu.VMEM((nseq, SUBLANES, wid), F32)],
        input_output_aliases={n_in + a: 7 + a for a in range(n_alias)},
        compiler_params=_compiler_params(("arbitrary",)),
        name=name,
    )(x, W['g_mix'], W['w_even_in'], c0, W['lru_conv_w'], W['lru_conv_b'], W['lbp'], *prev)


def _proj_odd_kernel(x_ref, g_ref, w_ref, wdt_ref, c0_ref, cw_ref, cb_ref, dtb_ref, *rest, nseq, t_len, tiles_per_seq,
                     n_alias):
    zs_ref, xs_ref, b_ref, c_ref, dt_ref, co_ref, hist_ref = rest[n_alias:]
    d_inner = zs_ref.shape[1]
    conv_dim = hist_ref.shape[2]
    bc_w = b_ref.shape[1]

    @pl.when(pl.program_id(0) % tiles_per_seq == 0)
    def _():
        _init_hist(hist_ref, c0_ref)

    xn = _rms(x_ref[...], g_ref[...]).astype(BF16)
    for c0 in range(0, d_inner, bc_w):
        zs_ref[:, c0:c0 + bc_w] = _silu(_dot(xn, w_ref[:, c0:c0 + bc_w]))
    for c0 in range(0, conv_dim, bc_w):
        cols = slice(c0, c0 + bc_w)
        conv, new_hist = _conv_step(_dot(xn, w_ref[:, d_inner + c0:d_inner + c0 + bc_w]), hist_ref.at[:, :, cols],
                                    cw_ref[:, cols], cb_ref[:, cols], nseq, t_len)
        co_ref[:, :, cols] = new_hist
        act = _silu(conv)
        if c0 < d_inner:
            xs_ref[:, cols] = act
        elif c0 == d_inner:
            b_ref[...] = act.astype(b_ref.dtype)
        else:
            c_ref[...] = act.astype(c_ref.dtype)
    dt_ref[...] = _softplus(_dot(xn, wdt_ref[...]) + dtb_ref[...])


def _proj_odd(x, c0, prev, W, layer, j, *, seq_len, tm, bc_dtype, name):
    m, d = x.shape
    conv_dim = c0.shape[-1]
    n_main = W['ssm_in'].shape[-1]
    head_lanes = W['ssm_dt_bias'].shape[-1]
    d_inner = n_main - conv_dim
    bc_w = (conv_dim - d_inner) // 2
    nseq, t_len, tiles_per_seq = _conv_tiling(m, seq_len, tm)
    row = lambda i: (i, 0)
    state_spec = pl.BlockSpec((None, nseq, HIST, conv_dim), lambda i: (j, i // tiles_per_seq, 0, 0))
    n_alias = len(prev)
    kern = functools.partial(_proj_odd_kernel, nseq=nseq, t_len=t_len, tiles_per_seq=tiles_per_seq,
                             n_alias=n_alias)
    n_in = 8
    return pl.pallas_call(
        kern,
        grid=(m // tm,),
        in_specs=[pl.BlockSpec((tm, d), row), _resident((1, d), layer), _resident((d, n_main), j),
                  _resident((d, head_lanes), j), state_spec,
                  _resident((CONV_W, conv_dim), j), _resident((1, conv_dim), j), _resident((1, head_lanes), j)]
                 + [pl.BlockSpec(memory_space=pl.ANY)] * n_alias,
        out_specs=[pl.BlockSpec((tm, d_inner), row), pl.BlockSpec((tm, d_inner), row),
                   pl.BlockSpec((tm, bc_w), row), pl.BlockSpec((tm, bc_w), row),
                   pl.BlockSpec((tm, head_lanes), row), state_spec],
        out_shape=[jax.ShapeDtypeStruct((m, d_inner), F32), jax.ShapeDtypeStruct((m, d_inner), F32),
                   jax.ShapeDtypeStruct((m, bc_w), bc_dtype), jax.ShapeDtypeStruct((m, bc_w), bc_dtype),
                   jax.ShapeDtypeStruct((m, head_lanes), F32), jax.ShapeDtypeStruct(c0.shape, F32)],
        scratch_shapes=[pltpu.VMEM((nseq, SUBLANES, conv_dim), F32)],
        input_output_aliases={n_in + a: 5 + a for a in range(n_alias)},
        compiler_params=_compiler_params(("arbitrary",)),
        name=name,
    )(x, W['g_mix'], W['ssm_in'], W['ssm_in_dt'], c0, W['ssm_conv_w'], W['ssm_conv_b'], W['ssm_dt_bias'], *prev)


def _post_kernel(*refs, n_act, final):
    x_ref, p_ref = refs[0], refs[1]
    act_refs = refs[2:2 + n_act]
    wo_refs = refs[2 + n_act:2 + 2 * n_act]
    gf_ref, w1_ref, w3_ref, w2_ref, wg_ref, wu_ref, gp_ref, gfin_ref, o_ref = refs[2 + 2 * n_act:]
    tm = x_ref.shape[0]
    rb = min(tm, POST_ROWS)
    for r0 in range(0, tm, rb):
        rs = slice(r0, r0 + rb)
        x = x_ref[rs, :]
        for act_ref, wo_ref in zip(act_refs, wo_refs):
            x = x + _dot(act_ref[rs, :].astype(BF16), wo_ref[...])
        xn = _rms(x, gf_ref[...]).astype(BF16)
        h = (_silu(_dot(xn, w1_ref[...])) * _dot(xn, w3_ref[...])).astype(BF16)
        x = x + _dot(h, w2_ref[...])
        gate = _sigmoid(_dot(x.astype(BF16), wg_ref[...]))
        emb = _dot(p_ref[rs, :].astype(BF16), wu_ref[...])
        x = x + _rms(gate * emb, gp_ref[...])
        if final:
            x = _rms(x, gfin_ref[...])
        o_ref[rs, :] = x


def _post_mixer(x, acts, p, wo, wo_layer, W, layer, final, tm, name):
    m, d = x.shape
    dp = p.shape[-1]
    dff = W['ffn_w1'].shape[-1]
    row = lambda i: (i, 0)
    act_specs, wo_specs, off = [], [], 0
    for a in acts:
        ka = a.shape[1]
        act_specs.append(pl.BlockSpec((tm, ka), row))
        wo_specs.append(pl.BlockSpec((None, ka, d), lambda i, blk=off // ka: (wo_layer, blk, 0),
                                     pipeline_mode=pl.Buffered(1)))
        off += ka
    return pl.pallas_call(
        functools.partial(_post_kernel, n_act=len(acts), final=final),
        grid=(m // tm,),
        in_specs=[pl.BlockSpec((tm, d), row), pl.BlockSpec((None, tm, dp), lambda i: (layer, i, 0))]
                 + act_specs + wo_specs + [
                  _resident((1, d), layer), _resident((d, dff), layer), _resident((d, dff), layer),
                  _resident((dff, d), layer), _resident((d, d), layer), _resident((dp, d), layer),
                  _resident((1, d), layer), _resident((1, d))],
        out_specs=pl.BlockSpec((tm, d), row),
        out_shape=jax.ShapeDtypeStruct((m, d), F32),
        compiler_params=_compiler_params(("parallel",)),
        name=name,
    )(x, p, *acts, *([wo] * len(acts)), W['g_ffn'], W['ffn_w1'], W['ffn_w3'], W['ffn_w2'], W['ple_gate'],
      W['ple_up'], W['g_ple'], W['g_final'])


def _even_kernel(q_ref, lf_ref, k_ref, v_ref, ga_ref, yb_ref, u_ref, s0_ref, h0_ref, gn_ref, wa_ref, ba_ref,
                 wx_ref, bx_ref, lam_ref, *rest, nseq, t_len, n_sub, fresh, n_alias):
    oa_ref, ob_ref, s_ref, h_ref = rest[n_alias:]
    rows = nseq * t_len
    wid = q_ref.shape[1]
    levels = t_len.bit_length() - 1
    tiles = t_len // SUBLANES
    step = pl.program_id(2)

    @pl.when(step == 0)
    def _():
        s_ref[...] = s0_ref[...]
        h_ref[...] = h0_ref[...]

    t_in = lax.broadcasted_iota(jnp.int32, (rows, wid), 0) & (t_len - 1)
    t_sub3 = lax.broadcasted_iota(jnp.int32, (rows // SUBLANES, SUBLANES, wid), 1)
    tri16 = _seq_tri(rows, t_len).astype(BF16)
    r_i = lax.broadcasted_iota(jnp.int32, (rows, rows), 0)
    c_i = lax.broadcasted_iota(jnp.int32, (rows, rows), 1)
    pair_level = jnp.where(c_i > r_i, -1, 32 - lax.clz(r_i ^ c_i))
    soft_lam = _softplus(-lam_ref[...])

    for sub in range(n_sub):
        rsl = slice(sub * rows, (sub + 1) * rows)
        u = u_ref[rsl, :]
        u16 = u.astype(BF16)
        r = _sigmoid(_dot(u16, wa_ref[...]) + ba_ref[...])
        gi = _sigmoid(_dot(u16, wx_ref[...]) + bx_ref[...])
        log_a = (-LRU_C) * r * soft_lam
        a = jnp.exp(log_a)
        mult = jnp.sqrt(-jnp.tanh(log_a) * (a * a + 1.0))
        if fresh and sub == 0:
            mult = jnp.where((t_in == 0) & (step == 0), 1.0, mult)
        bt = mult * (gi * u)
        a = a.reshape(rows // SUBLANES, SUBLANES, wid)
        bt = bt.reshape(rows // SUBLANES, SUBLANES, wid)
        shift = 1
        while shift < SUBLANES:
            valid = t_sub3 >= shift
            a_sh = pltpu.roll(a, shift, 1)
            b_sh = pltpu.roll(bt, shift, 1)
            bt = jnp.where(valid, a * b_sh + bt, bt)
            a = jnp.where(valid, a * a_sh, a)
            shift *= 2
        a4 = a.reshape(nseq, tiles, SUBLANES, wid)
        b4 = bt.reshape(nseq, tiles, SUBLANES, wid)
        carry = h_ref[...]
        h_tiles = []
        for tile in range(tiles):
            h_k = a4[:, tile] * carry + b4[:, tile]
            carry = h_k[:, SUBLANES - 1:SUBLANES, :]
            h_tiles.append(h_k)
        h_ref[...] = carry
        hseq = (h_tiles[0] if tiles == 1 else jnp.concatenate(h_tiles, axis=1)).reshape(rows, wid)
        ob_ref[rsl, :] = (yb_ref[rsl, :] * hseq).astype(ob_ref.dtype)

        logf = lf_ref[rsl, :]
        k = k_ref[rsl, :]
        q = q_ref[rsl, :]
        v16 = v_ref[rsl, :].astype(BF16)
        b = _prefix_sum(tri16, logf)
        b_last = _seq_last(b, nseq, t_len)
        qe16 = (q * jnp.exp(b)).astype(BF16)
        kd16 = (k * jnp.exp(b_last - b)).astype(BF16)
        q16 = q.astype(BF16)
        k16 = k.astype(BF16)

        att = jnp.where(pair_level == 0, _dot_nt(q16, k16), 0.0)
        for lvl in range(1, levels + 1):
            blk = 1 << lvl
            if lvl == 1:
                e = jnp.where((t_in & 1) == 1, logf, 0.0)
            elif lvl == 2:
                logf3 = logf.reshape(rows // SUBLANES, SUBLANES, wid)
                nxt = pltpu.roll(logf3, SUBLANES - 1, 1).reshape(rows, wid)
                prv = pltpu.roll(logf3, 1, 1).reshape(rows, wid)
                pos = t_in & 3
                e = jnp.where(pos == 0, nxt, jnp.where(pos == 1, 0.0, jnp.where(pos == 2, logf, logf + prv)))
            else:
                b3 = b.reshape(rows // blk, blk, wid)
                mid = jnp.broadcast_to(b3[:, blk // 2 - 1:blk // 2, :], b3.shape).reshape(rows, wid)
                e = -jnp.abs(b - mid)
            w16 = jnp.exp(e).astype(BF16)
            att = jnp.where(pair_level == lvl, _dot_nt(q16 * w16, k16 * w16), att)

        o = _dot(att.astype(BF16), v16)
        inter = []
        for i in range(nseq):
            rs = slice(i * t_len, (i + 1) * t_len)
            s_old = s_ref[i]
            inter.append(_dot(qe16[rs, :], s_old.astype(BF16)))
            e_last = jnp.exp(b[(i + 1) * t_len - 1:(i + 1) * t_len, :])
            scale = jnp.transpose(jnp.broadcast_to(e_last, (wid, wid)))
            s_ref[i] = scale * s_old + _dot_tn(kd16[rs, :], v16[rs, :])
        o = o + (inter[0] if nseq == 1 else jnp.concatenate(inter, axis=0))
        o = o * lax.rsqrt(jnp.mean(o * o, axis=-1, keepdims=True) + EPS)
        oa_ref[rsl, :] = (o * gn_ref[...] * ga_ref[rsl, :]).astype(oa_ref.dtype)


def _even_mixer(segs, states, prev, W, j, *, n_batch, seq_len, nseq, t_len, n_sub, fresh, act_dtype, name):
    s0, h0 = states
    n_heads, dk = s0.shape[2], s0.shape[3]
    wid = n_heads * dk
    rows_step = nseq * t_len * n_sub
    n_steps = seq_len // (t_len * n_sub)
    n_alias = len(prev)
    kern = functools.partial(_even_kernel, nseq=nseq, t_len=t_len, n_sub=n_sub, fresh=fresh, n_alias=n_alias)
    col = pl.BlockSpec((rows_step, dk), lambda b, h, s: (b * n_steps + s, h))
    par = pl.BlockSpec((None, 1, dk), lambda b, h, s: (j, 0, h))
    gate_w = pl.BlockSpec((None, None, dk, dk), lambda b, h, s: (j, h, 0, 0))
    state_specs = [pl.BlockSpec((None, nseq, None, dk, dk), lambda b, h, s: (j, b, h, 0, 0)),
                   pl.BlockSpec((None, nseq, 1, dk), lambda b, h, s: (j, b, 0, h))]
    n_in = 15
    return pl.pallas_call(
        kern,
        grid=(n_batch // nseq, n_heads, n_steps),
        in_specs=[col] * 7 + state_specs + [par, gate_w, par, gate_w, par, par]
                 + [pl.BlockSpec(memory_space=pl.ANY)] * n_alias,
        out_specs=[col, col] + state_specs,
        out_shape=[jax.ShapeDtypeStruct((n_batch * seq_len, wid), act_dtype),
                   jax.ShapeDtypeStruct((n_batch * seq_len, wid), act_dtype),
                   jax.ShapeDtypeStruct(s0.shape, F32),
                   jax.ShapeDtypeStruct(h0.shape, F32)],
        input_output_aliases={n_in + a: 2 + a for a in range(n_alias)},
        compiler_params=_compiler_params(("parallel", "parallel", "arbitrary")),
        name=name,
    )(*segs, s0, h0, W['hgrn_gnorm'], W['lru_wa'], W['lru_ba'], W['lru_wx'], W['lru_bx'], W['lru_lam'], *prev)


def _odd_kernel(zs_ref, xs_ref, bm_ref, cm_ref, dt_ref, s0_ref, alog_ref, dx_ref, gn_ref, ecols_ref, *rest,
                nseq, t_len, n_sub, n_steps, hpg, n_alias):
    act_ref, s_ref, st_ref = rest[n_alias:]
    carry_t = n_sub * n_steps > 1
    rows = nseq * t_len
    gcols = xs_ref.shape[1]
    p_dim = gcols // hpg
    n_state = bm_ref.shape[1]
    lanes = 2 * p_dim
    group = pl.program_id(1)
    step = pl.program_id(2)

    @pl.when(step == 0)
    def _():
        if carry_t:
            st_ref[...] = jnp.transpose(s0_ref[0])
        else:
            s_ref[...] = s0_ref[...]

    tri = _seq_tri(rows, t_len)
    tri16 = tri.astype(BF16)
    low = lax.broadcasted_iota(jnp.int32, (rows, lanes), 1) < p_dim
    head_lanes = dt_ref.shape[1]
    to_front = lax.rem(head_lanes - group * hpg, head_lanes)
    neg_a = -jnp.exp(alog_ref[...])

    for sub in range(n_sub):
        rsl = slice(sub * rows, (sub + 1) * rows)
        xs = xs_ref[rsl, :]
        b16 = bm_ref[rsl, :].astype(BF16)
        c16 = cm_ref[rsl, :].astype(BF16)
        dt_all = dt_ref[rsl, :]
        dt = pltpu.roll(dt_all, to_front, 1)
        la = pltpu.roll(dt_all * neg_a, to_front, 1)
        cum = _prefix_sum(tri16, la)
        cum_last = _seq_last(cum, nseq, t_len)
        cum_t = jnp.transpose(cum)
        dt_t = jnp.transpose(dt)
        e_cols = _dot(_split_cat(jnp.exp(cum)), ecols_ref[...])
        w_cols = _dot(_split_cat(jnp.exp(cum_last - cum) * dt), ecols_ref[...])

        cb_g = _dot_nt(c16, b16)
        if carry_t:
            y_int = _dot(c16, st_ref[...].astype(BF16))
        else:
            y_int = []
            for i in range(nseq):
                rs = slice(i * t_len, (i + 1) * t_len)
                y_int.append(_dot_nt(c16[rs, :], s_ref[i].astype(BF16)))
            y_int = y_int[0] if nseq == 1 else jnp.concatenate(y_int, axis=0)
        x_upd = (xs * w_cols).astype(BF16)
        y_cols = []
        for hp in range(hpg // 2):
            cols = slice(2 * hp * p_dim, (2 * hp + 2) * p_dim)
            x_pair = xs[:, cols]
            m_pair, x_blocks = [], []
            for half in range(2):
                h = 2 * hp + half
                cum_row = jnp.broadcast_to(cum_t[h:h + 1, :], (rows, rows))
                seg = jnp.transpose(cum_row) - cum_row
                decay = jnp.exp(jnp.where(tri, seg, -jnp.inf))
                m_pair.append((cb_g * (decay * jnp.broadcast_to(dt_t[h:h + 1, :], (rows, rows)))).astype(BF16))
                x_blocks.append(jnp.where(low if half == 0 else ~low, x_pair, 0.0).astype(BF16))
            y_cols.append(e_cols[:, cols] * y_int[:, cols] + dx_ref[:, cols] * x_pair
                          + _dot(jnp.concatenate(m_pair, axis=1), jnp.concatenate(x_blocks, axis=0)))
        if carry_t:
            st_ref[...] = st_ref[...] * e_cols[rows - 1:rows, :] + _dot_tn(b16, x_upd)
        for i in range(0 if carry_t else nseq):
            rs = slice(i * t_len, (i + 1) * t_len)
            upd = _dot_tn(x_upd[rs, :], b16[rs, :])
            last_t = jnp.transpose(jnp.broadcast_to(cum[(i + 1) * t_len - 1:(i + 1) * t_len, :],
                                                    (head_lanes, head_lanes)))
            for h in range(hpg):
                hr = slice(h * p_dim, (h + 1) * p_dim)
                dec = jnp.exp(jnp.broadcast_to(last_t[h:h + 1, 0:n_state], (p_dim, n_state)))
                s_ref[i, hr, :] = dec * s_ref[i, hr, :] + upd[hr, :]

        y = jnp.concatenate(y_cols, axis=1) * zs_ref[rsl, :]
        y = y * lax.rsqrt(jnp.mean(y * y, axis=-1, keepdims=True) + EPS)
        act_ref[rsl, :] = (y * gn_ref[...]).astype(act_ref.dtype)

    if carry_t:
        @pl.when(step == n_steps - 1)
        def _():
            s_ref[0] = jnp.transpose(st_ref[...])


def _odd_mixer(segs, s0, prev, W, j, *, n_batch, seq_len, nseq, t_len, n_sub, n_heads, act_dtype, name):
    zs, xs, bm, cm, dt = segs
    d_inner, n_state = s0.shape[2], s0.shape[3]
    n_groups = bm.shape[1] // n_state
    hpg = n_heads // n_groups
    gcols = d_inner // n_groups
    head_lanes = dt.shape[1]
    rows = nseq * t_len
    rows_step = rows * n_sub
    n_steps = seq_len // (t_len * n_sub)
    n_alias = len(prev)
    assert nseq == 1 or n_sub * n_steps == 1, "several sequences per step are swept in one sub-chunk"
    kern = functools.partial(_odd_kernel, nseq=nseq, t_len=t_len, n_sub=n_sub, n_steps=n_steps, hpg=hpg,
                             n_alias=n_alias)
    col = lambda width: pl.BlockSpec((rows_step, width), lambda b, g, s: (b * n_steps + s, g))
    par = pl.BlockSpec((None, 1, gcols), lambda b, g, s: (j, 0, g))
    state_spec = pl.BlockSpec((None, nseq, gcols, n_state), lambda b, g, s: (j, b, g, 0))
    head_to_cols = jnp.tile(jnp.repeat(jnp.eye(head_lanes, hpg, dtype=BF16), gcols // hpg, axis=1), (2, 1))
    n_in = 10
    return pl.pallas_call(
        kern,
        grid=(n_batch // nseq, n_groups, n_steps),
        in_specs=[col(gcols), col(gcols), col(n_state), col(n_state),
                  pl.BlockSpec((rows_step, head_lanes), lambda b, g, s: (b * n_steps + s, 0)), state_spec,
                  _resident((1, head_lanes), j), par, par, _resident(head_to_cols.shape)]
                 + [pl.BlockSpec(memory_space=pl.ANY)] * n_alias,
        out_specs=[col(gcols), state_spec],
        out_shape=[jax.ShapeDtypeStruct((n_batch * seq_len, d_inner), act_dtype),
                   jax.ShapeDtypeStruct(s0.shape, F32)],
        scratch_shapes=[pltpu.VMEM((n_state, gcols), F32)],
        input_output_aliases={n_in + a: 1 + a for a in range(n_alias)},
        compiler_params=_compiler_params(("parallel", "parallel", "arbitrary")),
        name=name,
    )(zs, xs, bm, cm, dt, s0, W['ssm_a_log'], W['ssm_dx'], W['ssm_gnorm'], head_to_cols, *prev)


def _trunk(x, p, even_states, odd_states, fresh, W, cfg, tag):
    n_batch, seq_len, d = x.shape
    m = n_batch * seq_len
    depth = p.shape[0]
    nseq, t_len, n_sub, tm_even, tm_odd, tm_post, act_dtype = cfg
    hgrn0, lru_h0, lru_conv0 = even_states
    ssm0, ssm_conv0 = odd_states
    x = x.reshape(m, d)
    p = p.reshape(depth, m, p.shape[-1])
    mixer_args = dict(n_batch=n_batch, seq_len=seq_len, nseq=nseq, t_len=t_len, n_sub=n_sub, act_dtype=act_dtype)
    even_out, even_conv, odd_out, odd_conv = (), (), (), ()
    for i in range(depth):
        j = i // 2
        if i % 2 == 0:
            *segs, conv = _proj_even(x, lru_conv0, even_conv, W, i, j, seq_len=seq_len, tm=tm_even,
                                     v_dtype=act_dtype, name=f"{tag}_proj{i}")
            even_conv = (conv,)
            act_a, act_b, *even_out = _even_mixer(segs, (hgrn0, lru_h0), even_out, W, j, fresh=fresh,
                                                  name=f"{tag}_even{i}", **mixer_args)
            acts, wo = (act_a, act_b), W['w_even_out']
        else:
            *segs, conv = _proj_odd(x, ssm_conv0, odd_conv, W, i, j, seq_len=seq_len, tm=tm_odd,
                                    bc_dtype=act_dtype, name=f"{tag}_proj{i}")
            odd_conv = (conv,)
            act, *odd_out = _odd_mixer(segs, ssm0, odd_out, W, j, n_heads=W['n_heads_c'],
                                       name=f"{tag}_odd{i}", **mixer_args)
            acts, wo = (act,), W['ssm_out']
        x = _post_mixer(x, acts, p, wo, j, W, i, i == depth - 1, tm_post, f"{tag}_post{i}")
    return x.reshape(n_batch, seq_len, d), even_out, even_conv[0], odd_out[0], odd_conv[0]


def kernel(x_prompt, x_sample, state_hgrn, state_lru_h, state_lru_conv, state_ssm, state_ssm_conv, p_prompt, p_sample, g_mix, g_ffn, g_ple, g_final, w_even_in, hgrn_lb, hgrn_gnorm, lru_conv_w, lru_conv_b, lru_wa, lru_ba, lru_wx, lru_bx, lru_lam, w_even_out, ssm_in, ssm_conv_w, ssm_conv_b, ssm_dt_bias, ssm_a_log, ssm_d, ssm_gnorm, ssm_out, ffn_w1, ffn_w3, ffn_w2, ple_up, ple_gate):
    n_even = state_hgrn.shape[0]
    n_odd, _, n_heads_c, p_c, n_c = state_ssm.shape
    wid = state_lru_h.shape[-1]
    d = x_prompt.shape[-1]
    d_inner = n_heads_c * p_c

    lb = jnp.cumsum(jax.nn.softmax(hgrn_lb.astype(F32), axis=0), axis=0)
    lb = lb - lb[0]
    lbp = jnp.stack([jnp.log(lb), jnp.log1p(-lb), 1.0 - lb], axis=1)
    n_main = ssm_in.shape[-1] - n_heads_c
    head_pad = -(-n_heads_c // LANES) * LANES
    pad_h = lambda a: jnp.pad(a.astype(F32), ((0, 0), (0, head_pad - n_heads_c))).reshape(n_odd, 1, head_pad)
    vec = lambda a: a.astype(F32).reshape(a.shape[0], 1, -1)
    W = dict(
        g_mix=vec(g_mix), g_ffn=vec(g_ffn), g_ple=vec(g_ple), g_final=g_final.reshape(1, d),
        lbp=lbp, hgrn_gnorm=vec(hgrn_gnorm), lru_conv_w=lru_conv_w, lru_conv_b=vec(lru_conv_b),
        lru_ba=vec(lru_ba), lru_bx=vec(lru_bx), lru_lam=vec(lru_lam),
        ssm_conv_w=ssm_conv_w, ssm_conv_b=vec(ssm_conv_b), ssm_gnorm=vec(ssm_gnorm),
        ssm_dt_bias=pad_h(ssm_dt_bias), ssm_a_log=pad_h(ssm_a_log),
        ssm_dx=jnp.repeat(ssm_d.astype(F32), p_c, axis=1).reshape(n_odd, 1, d_inner),
        w_even_in=w_even_in.astype(BF16), lru_wa=lru_wa.astype(BF16), lru_wx=lru_wx.astype(BF16),
        w_even_out=w_even_out.astype(BF16),
        ssm_in=ssm_in[..., :n_main].astype(BF16),
        ssm_in_dt=jnp.pad(ssm_in[..., n_main:].astype(BF16), ((0, 0), (0, 0), (0, head_pad - n_heads_c))),
        ssm_out=ssm_out.astype(BF16), ffn_w1=ffn_w1.astype(BF16), ffn_w3=ffn_w3.astype(BF16),
        ffn_w2=ffn_w2.astype(BF16), ple_up=ple_up.astype(BF16), ple_gate=ple_gate.astype(BF16),
        n_heads_c=n_heads_c)

    def run(x, p, hgrn, lru_h, lru_conv, ssm, ssm_conv, fresh, cfg, tag):
        nb = x.shape[0]
        even_states = (hgrn, lru_h.reshape(n_even, nb, 1, wid), lru_conv)
        odd_states = (ssm.reshape(n_odd, nb, d_inner, n_c), ssm_conv)
        y, (hg, lh), lc, ss, sc = _trunk(x, p, even_states, odd_states, fresh, W, cfg, tag)
        return y, hg, lh.reshape(n_even, nb, wid), lc, ss.reshape(n_odd, nb, n_heads_c, p_c, n_c), sc

    bp = x_prompt.shape[0]
    zeros = lambda ref: jnp.zeros((ref.shape[0], bp) + ref.shape[2:], F32)
    cfg_prompt = (1, 128, 16, 1024, 512, 512, BF16)
    cfg_sample = (16, x_sample.shape[1], 1, 512, 512, 512, F32)
    y_p, hg_p, lh_p, lc_p, ss_p, sc_p = run(
        x_prompt, p_prompt, zeros(state_hgrn), zeros(state_lru_h), zeros(state_lru_conv), zeros(state_ssm),
        zeros(state_ssm_conv), True, cfg_prompt, "prompt")
    y_s, hg_s, lh_s, lc_s, ss_s, sc_s = run(
        x_sample, p_sample, state_hgrn, state_lru_h, state_lru_conv, state_ssm, state_ssm_conv, False,
        cfg_sample, "sample")
    return (y_p, y_s, hg_p, hg_s, lh_p, lh_s, lc_p, lc_s, ss_p, ss_s, sc_p, sc_s)
```

```python
import functools
import math

import jax
import jax.numpy as jnp
from jax import lax
from jax.experimental import pallas as pl
from jax.experimental.pallas import tpu as pltpu

F32 = jnp.float32
BF16 = jnp.bfloat16
EPS = 1e-6
LRU_C = 8.0
CONV_W = 4
HIST = CONV_W - 1
SUBLANES = 8
LANES = 128
INT_BITS = 32
LOG2_E = math.log2(math.e)
VMEM_LIMIT = 56 * 1024 * 1024
POST_ROWS = 256


def _dot(a, b):
    return jnp.dot(a, b, preferred_element_type=F32)


def _dot_tn(a, b):
    return lax.dot_general(a, b, (((0,), (0,)), ((), ())), preferred_element_type=F32)


def _dot_nt(a, b):
    return lax.dot_general(a, b, (((1,), (1,)), ((), ())), preferred_element_type=F32)


def _split_cat(x):
    hi, lo = _split_rows(x)
    return jnp.concatenate([hi, lo], axis=1)


def _prefix_sum(tri16, x):
    return _dot(jnp.concatenate([tri16, tri16], axis=1), jnp.concatenate(_split_rows(x), axis=0))


def _split_rows(x):
    hi = x.astype(BF16)
    lo = (x - hi.astype(F32)).astype(BF16)
    return [hi, lo]


def _rms(x, g):
    return x * lax.rsqrt(jnp.mean(x * x, axis=-1, keepdims=True) + EPS) * g


def _sigmoid(x):
    return 0.5 * jnp.tanh(0.5 * x) + 0.5


def _silu(x):
    half = 0.5 * x
    return half * jnp.tanh(half) + half


def _softplus(x):
    return jnp.maximum(x, 0.0) + jnp.log1p(jnp.exp(-jnp.abs(x)))


def _log_sigmoid(x):
    return jnp.minimum(x, 0.0) - jnp.log(1.0 + jnp.exp(-jnp.abs(x)))


def _logaddexp(a, b):
    return jnp.maximum(a, b) + jnp.log(1.0 + jnp.exp(-jnp.abs(a - b)))


def _gelu_tanh(x):
    return 0.5 * x * (1.0 + jnp.tanh(math.sqrt(2.0 / math.pi) * (x + 0.044715 * (x * x * x))))


def _seq_tri(rows, t_len):
    shift = t_len.bit_length() - 1
    r = lax.broadcasted_iota(jnp.int32, (rows, rows), 0)
    c = lax.broadcasted_iota(jnp.int32, (rows, rows), 1)
    return ((r >> shift) == (c >> shift)) & (c <= r)


def _seq_last(x, nseq, t_len):
    x3 = x.reshape(nseq, t_len, x.shape[-1])
    last = x3[:, t_len - 1:t_len, :]
    return jnp.broadcast_to(last, x3.shape).reshape(x.shape)


def _init_hist(hist_ref, c0_ref):
    nseq, _, ch = hist_ref.shape
    hist_ref[:, :SUBLANES - HIST, :] = jnp.zeros((nseq, SUBLANES - HIST, ch), F32)
    hist_ref[:, SUBLANES - HIST:, :] = c0_ref[...]


def _shift_rows(cur, hist, j, nseq, t_len):
    rows, ch = cur.shape
    row = lax.broadcasted_iota(jnp.int32, (nseq, SUBLANES, ch), 1)
    rolled = pltpu.roll(cur, j, 0).reshape(nseq, t_len, ch)
    hist_j = pltpu.roll(hist, (nseq * SUBLANES + j - SUBLANES) % (nseq * SUBLANES), 0)
    head = jnp.where(row < j, hist_j.reshape(nseq, SUBLANES, ch), rolled[:, 0:SUBLANES, :])
    shifted = head if t_len == SUBLANES else jnp.concatenate([head, rolled[:, SUBLANES:, :]], axis=1)
    return shifted.reshape(rows, ch)


def _conv_step(raw, hist_ref, w, bias, nseq, t_len):
    assert CONV_W == 4
    raw3 = raw.reshape(nseq, t_len, raw.shape[1])
    hist = hist_ref[...].reshape(nseq * SUBLANES, raw.shape[1])
    w0, w1, w2, w3 = (w[k:k + 1, :] for k in range(CONV_W))
    prev = _shift_rows(raw, hist, 1, nseq, t_len)
    pair = raw * w1 + prev * w0
    pair_hist = hist * w1 + pltpu.roll(hist, 1, 0) * w0
    out = raw * w3 + bias + prev * w2 + _shift_rows(pair, pair_hist, 2, nseq, t_len)
    hist_ref[...] = raw3[:, t_len - SUBLANES:, :]
    return out, raw3[:, t_len - HIST:, :]


def _resident(shape, layer=None):
    nd = len(shape)
    if layer is None:
        return pl.BlockSpec(shape, lambda *_: (0,) * nd, pipeline_mode=pl.Buffered(1))
    return pl.BlockSpec((None,) + tuple(shape), lambda *_: (layer,) + (0,) * nd, pipeline_mode=pl.Buffered(1))


def _compiler_params(semantics):
    return pltpu.CompilerParams(dimension_semantics=semantics, vmem_limit_bytes=VMEM_LIMIT)


def _conv_tiling(m, seq_len, tm):
    if tm >= seq_len:
        return tm // seq_len, seq_len, 1
    return 1, tm, seq_len // tm


def _proj_even_kernel(x_ref, g_ref, w_ref, c0_ref, cw_ref, cb_ref, lbp_ref, *rest, nseq, t_len, tiles_per_seq,
                      n_alias):
    q_ref, lf_ref, k_ref, v_ref, ga_ref, yb_ref, u_ref, co_ref, hist_ref = rest[n_alias:]
    wid = q_ref.shape[1]

    @pl.when(pl.program_id(0) % tiles_per_seq == 0)
    def _():
        _init_hist(hist_ref, c0_ref)

    lbp = lbp_ref[...]
    xn = _rms(x_ref[...], g_ref[...]).astype(BF16)
    seg = lambda s: _dot(xn, w_ref[:, s * wid:(s + 1) * wid])
    q_ref[...] = _silu(seg(0))
    fz = seg(1)
    lf_ref[...] = _logaddexp(lbp[0:1, :], lbp[1:2, :] + _log_sigmoid(fz))
    k_ref[...] = lbp[2:3, :] * _sigmoid(-fz)
    v_ref[...] = seg(2).astype(v_ref.dtype)
    ga_ref[...] = _silu(seg(3))
    yb_ref[...] = _gelu_tanh(seg(4))
    u, new_hist = _conv_step(seg(5), hist_ref, cw_ref[...], cb_ref[...], nseq, t_len)
    u_ref[...] = u
    co_ref[...] = new_hist


def _proj_even(x, c0, prev, W, layer, j, *, seq_len, tm, v_dtype, name):
    m, d = x.shape
    wid = c0.shape[-1]
    nseq, t_len, tiles_per_seq = _conv_tiling(m, seq_len, tm)
    row = lambda i: (i, 0)
    state_spec = pl.BlockSpec((None, nseq, HIST, wid), lambda i: (j, i // tiles_per_seq, 0, 0))
    seg = pl.BlockSpec((tm, wid), row)
    n_alias = len(prev)
    kern = functools.partial(_proj_even_kernel, nseq=nseq, t_len=t_len, tiles_per_seq=tiles_per_seq,
                             n_alias=n_alias)
    n_in = 7
    return pl.pallas_call(
        kern,
        grid=(m // tm,),
        in_specs=[pl.BlockSpec((tm, d), row), _resident((1, d), layer), _resident((d, 6 * wid), j), state_spec,
                  _resident((CONV_W, wid), j), _resident((1, wid), j), _resident((3, wid), j)]
                 + [pl.BlockSpec(memory_space=pl.ANY)] * n_alias,
        out_specs=[seg] * 7 + [state_spec],
        out_shape=[jax.ShapeDtypeStruct((m, wid), dt) for dt in (F32, F32, F32, v_dtype, F32, F32, F32)]
                  + [jax.ShapeDtypeStruct(c0.shape, F32)],
        scratch_shapes=[pltpu.VMEM((nseq, SUBLANES, wid), F32)],
        input_output_aliases={n_in + a: 7 + a for a in range(n_alias)},
        compiler_params=_compiler_params(("arbitrary",)),
        name=name,
    )(x, W['g_mix'], W['w_even_in'], c0, W['lru_conv_w'], W['lru_conv_b'], W['lbp'], *prev)


def _proj_odd_kernel(x_ref, g_ref, w_ref, wdt_ref, c0_ref, cw_ref, cb_ref, dtb_ref, *rest, nseq, t_len, tiles_per_seq,
                     n_alias):
    zs_ref, xs_ref, b_ref, c_ref, dt_ref, co_ref, hist_ref = rest[n_alias:]
    d_inner = zs_ref.shape[1]
    conv_dim = hist_ref.shape[2]
    bc_w = b_ref.shape[1]

    @pl.when(pl.program_id(0) % tiles_per_seq == 0)
    def _():
        _init_hist(hist_ref, c0_ref)

    xn = _rms(x_ref[...], g_ref[...]).astype(BF16)
    for c0 in range(0, d_inner, bc_w):
        zs_ref[:, c0:c0 + bc_w] = _silu(_dot(xn, w_ref[:, c0:c0 + bc_w]))
    for c0 in range(0, conv_dim, bc_w):
        cols = slice(c0, c0 + bc_w)
        conv, new_hist = _conv_step(_dot(xn, w_ref[:, d_inner + c0:d_inner + c0 + bc_w]), hist_ref.at[:, :, cols],
                                    cw_ref[:, cols], cb_ref[:, cols], nseq, t_len)
        co_ref[:, :, cols] = new_hist
        act = _silu(conv)
        if c0 < d_inner:
            xs_ref[:, cols] = act
        elif c0 == d_inner:
            b_ref[...] = act.astype(b_ref.dtype)
        else:
            c_ref[...] = act.astype(c_ref.dtype)
    dt_ref[...] = _softplus(_dot(xn, wdt_ref[...]) + dtb_ref[...])


def _proj_odd(x, c0, prev, W, layer, j, *, seq_len, tm, bc_dtype, name):
    m, d = x.shape
    conv_dim = c0.shape[-1]
    n_main = W['ssm_in'].shape[-1]
    head_lanes = W['ssm_dt_bias'].shape[-1]
    d_inner = n_main - conv_dim
    bc_w = (conv_dim - d_inner) // 2
    nseq, t_len, tiles_per_seq = _conv_tiling(m, seq_len, tm)
    row = lambda i: (i, 0)
    state_spec = pl.BlockSpec((None, nseq, HIST, conv_dim), lambda i: (j, i // tiles_per_seq, 0, 0))
    n_alias = len(prev)
    kern = functools.partial(_proj_odd_kernel, nseq=nseq, t_len=t_len, tiles_per_seq=tiles_per_seq,
                             n_alias=n_alias)
    n_in = 8
    return pl.pallas_call(
        kern,
        grid=(m // tm,),
        in_specs=[pl.BlockSpec((tm, d), row), _resident((1, d), layer), _resident((d, n_main), j),
                  _resident((d, head_lanes), j), state_spec,
                  _resident((CONV_W, conv_dim), j), _resident((1, conv_dim), j), _resident((1, head_lanes), j)]
                 + [pl.BlockSpec(memory_space=pl.ANY)] * n_alias,
        out_specs=[pl.BlockSpec((tm, d_inner), row), pl.BlockSpec((tm, d_inner), row),
                   pl.BlockSpec((tm, bc_w), row), pl.BlockSpec((tm, bc_w), row),
                   pl.BlockSpec((tm, head_lanes), row), state_spec],
        out_shape=[jax.ShapeDtypeStruct((m, d_inner), F32), jax.ShapeDtypeStruct((m, d_inner), F32),
                   jax.ShapeDtypeStruct((m, bc_w), bc_dtype), jax.ShapeDtypeStruct((m, bc_w), bc_dtype),
                   jax.ShapeDtypeStruct((m, head_lanes), F32), jax.ShapeDtypeStruct(c0.shape, F32)],
        scratch_shapes=[pltpu.VMEM((nseq, SUBLANES, conv_dim), F32)],
        input_output_aliases={n_in + a: 5 + a for a in range(n_alias)},
        compiler_params=_compiler_params(("arbitrary",)),
        name=name,
    )(x, W['g_mix'], W['ssm_in'], W['ssm_in_dt'], c0, W['ssm_conv_w'], W['ssm_conv_b'], W['ssm_dt_bias'], *prev)


def _post_kernel(*refs, n_act, final):
    x_ref, p_ref = refs[0], refs[1]
    act_refs = refs[2:2 + n_act]
    wo_refs = refs[2 + n_act:2 + 2 * n_act]
    gf_ref, w1_ref, w3_ref, w2_ref, wg_ref, wu_ref, gp_ref, gfin_ref, o_ref = refs[2 + 2 * n_act:]
    tm = x_ref.shape[0]
    rb = min(tm, POST_ROWS)
    for r0 in range(0, tm, rb):
        rs = slice(r0, r0 + rb)
        x = x_ref[rs, :]
        for act_ref, wo_ref in zip(act_refs, wo_refs):
            x = x + _dot(act_ref[rs, :].astype(BF16), wo_ref[...])
        xn = _rms(x, gf_ref[...]).astype(BF16)
        h = (_silu(_dot(xn, w1_ref[...])) * _dot(xn, w3_ref[...])).astype(BF16)
        x = x + _dot(h, w2_ref[...])
        gate = _sigmoid(_dot(x.astype(BF16), wg_ref[...]))
        emb = _dot(p_ref[rs, :].astype(BF16), wu_ref[...])
        x = x + _rms(gate * emb, gp_ref[...])
        if final:
            x = _rms(x, gfin_ref[...])
        o_ref[rs, :] = x


def _post_mixer(x, acts, p, wo, wo_layer, W, layer, final, tm, name):
    m, d = x.shape
    dp = p.shape[-1]
    dff = W['ffn_w1'].shape[-1]
    row = lambda i: (i, 0)
    act_specs, wo_specs, off = [], [], 0
    for a in acts:
        ka = a.shape[1]
        act_specs.append(pl.BlockSpec((tm, ka), row))
        wo_specs.append(pl.BlockSpec((None, ka, d), lambda i, blk=off // ka: (wo_layer, blk, 0),
                                     pipeline_mode=pl.Buffered(1)))
        off += ka
    return pl.pallas_call(
        functools.partial(_post_kernel, n_act=len(acts), final=final),
        grid=(m // tm,),
        in_specs=[pl.BlockSpec((tm, d), row), pl.BlockSpec((None, tm, dp), lambda i: (layer, i, 0))]
                 + act_specs + wo_specs + [
                  _resident((1, d), layer), _resident((d, dff), layer), _resident((d, dff), layer),
                  _resident((dff, d), layer), _resident((d, d), layer), _resident((dp, d), layer),
                  _resident((1, d), layer), _resident((1, d))],
        out_specs=pl.BlockSpec((tm, d), row),
        out_shape=jax.ShapeDtypeStruct((m, d), F32),
        compiler_params=_compiler_params(("parallel",)),
        name=name,
    )(x, p, *acts, *([wo] * len(acts)), W['g_ffn'], W['ffn_w1'], W['ffn_w3'], W['ffn_w2'], W['ple_gate'],
      W['ple_up'], W['g_ple'], W['g_final'])


def _even_kernel(q_ref, lf_ref, k_ref, v_ref, ga_ref, yb_ref, u_ref, s0_ref, h0_ref, gn_ref, wa_ref, ba_ref,
                 wx_ref, bx_ref, lam_ref, *rest, nseq, t_len, n_sub, fresh, n_alias):
    oa_ref, ob_ref, s_ref, h_ref = rest[n_alias:]
    rows = nseq * t_len
    wid = q_ref.shape[1]
    levels = t_len.bit_length() - 1
    tiles = t_len // SUBLANES
    step = pl.program_id(2)

    @pl.when(step == 0)
    def _():
        s_ref[...] = s0_ref[...]
        h_ref[...] = h0_ref[...]

    t_in = lax.broadcasted_iota(jnp.int32, (rows, wid), 0) & (t_len - 1)
    t_sub3 = lax.broadcasted_iota(jnp.int32, (rows // SUBLANES, SUBLANES, wid), 1)
    tri16 = _seq_tri(rows, t_len).astype(BF16)
    r_i = lax.broadcasted_iota(jnp.int32, (rows, rows), 0)
    c_i = lax.broadcasted_iota(jnp.int32, (rows, rows), 1)
    pair_level = jnp.where(c_i > r_i, -1, INT_BITS - lax.clz(r_i ^ c_i))
    level_masks = [pair_level == lvl for lvl in range(levels + 1)]
    soft_lam = _softplus(-lam_ref[...])

    for sub in range(n_sub):
        rsl = slice(sub * rows, (sub + 1) * rows)
        u = u_ref[rsl, :]
        u16 = u.astype(BF16)
        r = _sigmoid(_dot(u16, wa_ref[...]) + ba_ref[...])
        gi = _sigmoid(_dot(u16, wx_ref[...]) + bx_ref[...])
        log_a = (-LRU_C) * r * soft_lam
        a = jnp.exp(log_a)
        mult = jnp.sqrt(-jnp.tanh(log_a) * (a * a + 1.0))
        if fresh and sub == 0:
            mult = jnp.where((t_in == 0) & (step == 0), 1.0, mult)
        bt = mult * (gi * u)
        a = a.reshape(rows // SUBLANES, SUBLANES, wid)
        bt = bt.reshape(rows // SUBLANES, SUBLANES, wid)
        shift = 1
        while shift < SUBLANES:
            valid = t_sub3 >= shift
            a_sh = pltpu.roll(a, shift, 1)
            b_sh = pltpu.roll(bt, shift, 1)
            bt = jnp.where(valid, a * b_sh + bt, bt)
            a = jnp.where(valid, a * a_sh, a)
            shift *= 2
        a4 = a.reshape(nseq, tiles, SUBLANES, wid)
        b4 = bt.reshape(nseq, tiles, SUBLANES, wid)
        carry = h_ref[...]
        h_tiles = []
        for tile in range(tiles):
            h_k = a4[:, tile] * carry + b4[:, tile]
            carry = h_k[:, SUBLANES - 1:SUBLANES, :]
            h_tiles.append(h_k)
        h_ref[...] = carry
        hseq = (h_tiles[0] if tiles == 1 else jnp.concatenate(h_tiles, axis=1)).reshape(rows, wid)
        ob_ref[rsl, :] = (yb_ref[rsl, :] * hseq).astype(ob_ref.dtype)

        logf = lf_ref[rsl, :]
        k = k_ref[rsl, :]
        q = q_ref[rsl, :]
        v16 = v_ref[rsl, :].astype(BF16)
        b = _prefix_sum(tri16, logf)
        b_last = _seq_last(b, nseq, t_len)
        qe16 = (q * jnp.exp(b)).astype(BF16)
        kd16 = (k * jnp.exp(b_last - b)).astype(BF16)
        q16 = q.astype(BF16)
        k16 = k.astype(BF16)

        att = jnp.where(level_masks[0], _dot_nt(q16, k16), 0.0)
        for lvl in range(1, levels + 1):
            blk = 1 << lvl
            if lvl == 1:
                w = jnp.exp(jnp.where((t_in & 1) == 1, logf, 0.0))
            elif lvl == 2:
                logf3 = logf.reshape(rows // SUBLANES, SUBLANES, wid)
                nxt = pltpu.roll(logf3, SUBLANES - 1, 1).reshape(rows, wid)
                prv = pltpu.roll(logf3, 1, 1).reshape(rows, wid)
                pos = t_in & 3
                w = jnp.exp(jnp.where(pos == 0, nxt, jnp.where(pos == 1, 0.0, jnp.where(pos == 2, logf, logf + prv))))
            else:
                b3 = b.reshape(rows // blk, blk, wid)
                mid = jnp.broadcast_to(b3[:, blk // 2 - 1:blk // 2, :], b3.shape).reshape(rows, wid)
                w = jnp.exp2(jnp.abs(b - mid) * (-LOG2_E))
            w16 = w.astype(BF16)
            att = jnp.where(level_masks[lvl], _dot_nt(q16 * w16, k16 * w16), att)

        o = _dot(att.astype(BF16), v16)
        inter = []
        for i in range(nseq):
            rs = slice(i * t_len, (i + 1) * t_len)
            s_old = s_ref[i]
            inter.append(_dot(qe16[rs, :], s_old.astype(BF16)))
            e_last = jnp.exp(b[(i + 1) * t_len - 1:(i + 1) * t_len, :])
            scale = jnp.transpose(jnp.broadcast_to(e_last, (wid, wid)))
            s_ref[i] = scale * s_old + _dot_tn(kd16[rs, :], v16[rs, :])
        o = o + (inter[0] if nseq == 1 else jnp.concatenate(inter, axis=0))
        o = o * lax.rsqrt(jnp.mean(o * o, axis=-1, keepdims=True) + EPS)
        oa_ref[rsl, :] = (o * gn_ref[...] * ga_ref[rsl, :]).astype(oa_ref.dtype)


def _even_mixer(segs, states, prev, W, j, *, n_batch, seq_len, nseq, t_len, n_sub, fresh, act_dtype, name):
    s0, h0 = states
    n_heads, dk = s0.shape[2], s0.shape[3]
    wid = n_heads * dk
    rows_step = nseq * t_len * n_sub
    n_steps = seq_len // (t_len * n_sub)
    n_alias = len(prev)
    kern = functools.partial(_even_kernel, nseq=nseq, t_len=t_len, n_sub=n_sub, fresh=fresh, n_alias=n_alias)
    col = pl.BlockSpec((rows_step, dk), lambda b, h, s: (b * n_steps + s, h))
    par = pl.BlockSpec((None, 1, dk), lambda b, h, s: (j, 0, h))
    gate_w = pl.BlockSpec((None, None, dk, dk), lambda b, h, s: (j, h, 0, 0))
    state_specs = [pl.BlockSpec((None, nseq, None, dk, dk), lambda b, h, s: (j, b, h, 0, 0)),
                   pl.BlockSpec((None, nseq, 1, dk), lambda b, h, s: (j, b, 0, h))]
    n_in = 15
    return pl.pallas_call(
        kern,
        grid=(n_batch // nseq, n_heads, n_steps),
        in_specs=[col] * 7 + state_specs + [par, gate_w, par, gate_w, par, par]
                 + [pl.BlockSpec(memory_space=pl.ANY)] * n_alias,
        out_specs=[col, col] + state_specs,
        out_shape=[jax.ShapeDtypeStruct((n_batch * seq_len, wid), act_dtype),
                   jax.ShapeDtypeStruct((n_batch * seq_len, wid), act_dtype),
                   jax.ShapeDtypeStruct(s0.shape, F32),
                   jax.ShapeDtypeStruct(h0.shape, F32)],
        input_output_aliases={n_in + a: 2 + a for a in range(n_alias)},
        compiler_params=_compiler_params(("parallel", "parallel", "arbitrary")),
        name=name,
    )(*segs, s0, h0, W['hgrn_gnorm'], W['lru_wa'], W['lru_ba'], W['lru_wx'], W['lru_bx'], W['lru_lam'], *prev)


def _odd_kernel(zs_ref, xs_ref, bm_ref, cm_ref, dt_ref, s0_ref, alog_ref, dx_ref, gn_ref, ecols_ref, *rest,
                nseq, t_len, n_sub, n_steps, hpg, n_alias):
    act_ref, s_ref, st_ref = rest[n_alias:]
    carry_t = n_sub * n_steps > 1
    rows = nseq * t_len
    gcols = xs_ref.shape[1]
    p_dim = gcols // hpg
    n_state = bm_ref.shape[1]
    lanes = 2 * p_dim
    group = pl.program_id(1)
    step = pl.program_id(2)

    @pl.when(step == 0)
    def _():
        if carry_t:
            st_ref[...] = jnp.transpose(s0_ref[0])
        else:
            s_ref[...] = s0_ref[...]

    tri = _seq_tri(rows, t_len)
    tri16 = tri.astype(BF16)
    low = lax.broadcasted_iota(jnp.int32, (rows, lanes), 1) < p_dim
    head_lanes = dt_ref.shape[1]
    to_front = lax.rem(head_lanes - group * hpg, head_lanes)
    neg_a = -jnp.exp(alog_ref[...])

    for sub in range(n_sub):
        rsl = slice(sub * rows, (sub + 1) * rows)
        xs = xs_ref[rsl, :]
        b16 = bm_ref[rsl, :].astype(BF16)
        c16 = cm_ref[rsl, :].astype(BF16)
        dt_all = dt_ref[rsl, :]
        dt = pltpu.roll(dt_all, to_front, 1)
        la = pltpu.roll(dt_all * neg_a, to_front, 1)
        cum = _prefix_sum(tri16, la)
        cum_last = _seq_last(cum, nseq, t_len)
        cum_t = jnp.transpose(cum)
        dt_t = jnp.transpose(dt)
        e_cols = _dot(_split_cat(jnp.exp(cum)), ecols_ref[...])
        w_cols = _dot(_split_cat(jnp.exp(cum_last - cum) * dt), ecols_ref[...])

        cb_g = _dot_nt(c16, b16)
        if carry_t:
            y_int = _dot(c16, st_ref[...].astype(BF16))
        else:
            y_int = []
            for i in range(nseq):
                rs = slice(i * t_len, (i + 1) * t_len)
                y_int.append(_dot_nt(c16[rs, :], s_ref[i].astype(BF16)))
            y_int = y_int[0] if nseq == 1 else jnp.concatenate(y_int, axis=0)
        x_upd = (xs * w_cols).astype(BF16)
        y_cols = []
        for hp in range(hpg // 2):
            cols = slice(2 * hp * p_dim, (2 * hp + 2) * p_dim)
            x_pair = xs[:, cols]
            m_pair, x_blocks = [], []
            for half in range(2):
                h = 2 * hp + half
                cum_row = jnp.broadcast_to(cum_t[h:h + 1, :], (rows, rows))
                seg = jnp.transpose(cum_row) - cum_row
                decay = jnp.exp(jnp.where(tri, seg, -jnp.inf))
                m_pair.append((cb_g * (decay * jnp.broadcast_to(dt_t[h:h + 1, :], (rows, rows)))).astype(BF16))
                x_blocks.append(jnp.where(low if half == 0 else ~low, x_pair, 0.0).astype(BF16))
            y_cols.append(e_cols[:, cols] * y_int[:, cols] + dx_ref[:, cols] * x_pair
                          + _dot(jnp.concatenate(m_pair, axis=1), jnp.concatenate(x_blocks, axis=0)))
        if carry_t:
            st_ref[...] = st_ref[...] * e_cols[rows - 1:rows, :] + _dot_tn(b16, x_upd)
        for i in range(0 if carry_t else nseq):
            rs = slice(i * t_len, (i + 1) * t_len)
            upd = _dot_tn(x_upd[rs, :], b16[rs, :])
            last_t = jnp.transpose(jnp.broadcast_to(cum[(i + 1) * t_len - 1:(i + 1) * t_len, :],
                                                    (head_lanes, head_lanes)))
            for h in range(hpg):
                hr = slice(h * p_dim, (h + 1) * p_dim)
                dec = jnp.exp(jnp.broadcast_to(last_t[h:h + 1, 0:n_state], (p_dim, n_state)))
                s_ref[i, hr, :] = dec * s_ref[i, hr, :] + upd[hr, :]

        y = jnp.concatenate(y_cols, axis=1) * zs_ref[rsl, :]
        y = y * lax.rsqrt(jnp.mean(y * y, axis=-1, keepdims=True) + EPS)
        act_ref[rsl, :] = (y * gn_ref[...]).astype(act_ref.dtype)

    if carry_t:
        @pl.when(step == n_steps - 1)
        def _():
            s_ref[0] = jnp.transpose(st_ref[...])


def _odd_mixer(segs, s0, prev, W, j, *, n_batch, seq_len, nseq, t_len, n_sub, n_heads, act_dtype, name):
    zs, xs, bm, cm, dt = segs
    d_inner, n_state = s0.shape[2], s0.shape[3]
    n_groups = bm.shape[1] // n_state
    hpg = n_heads // n_groups
    gcols = d_inner // n_groups
    head_lanes = dt.shape[1]
    rows = nseq * t_len
    rows_step = rows * n_sub
    n_steps = seq_len // (t_len * n_sub)
    n_alias = len(prev)
    assert nseq == 1 or n_sub * n_steps == 1, "several sequences per step are swept in one sub-chunk"
    kern = functools.partial(_odd_kernel, nseq=nseq, t_len=t_len, n_sub=n_sub, n_steps=n_steps, hpg=hpg,
                             n_alias=n_alias)
    col = lambda width: pl.BlockSpec((rows_step, width), lambda b, g, s: (b * n_steps + s, g))
    par = pl.BlockSpec((None, 1, gcols), lambda b, g, s: (j, 0, g))
    state_spec = pl.BlockSpec((None, nseq, gcols, n_state), lambda b, g, s: (j, b, g, 0))
    head_to_cols = jnp.tile(jnp.repeat(jnp.eye(head_lanes, hpg, dtype=BF16), gcols // hpg, axis=1), (2, 1))
    n_in = 10
    return pl.pallas_call(
        kern,
        grid=(n_batch // nseq, n_groups, n_steps),
        in_specs=[col(gcols), col(gcols), col(n_state), col(n_state),
                  pl.BlockSpec((rows_step, head_lanes), lambda b, g, s: (b * n_steps + s, 0)), state_spec,
                  _resident((1, head_lanes), j), par, par, _resident(head_to_cols.shape)]
                 + [pl.BlockSpec(memory_space=pl.ANY)] * n_alias,
        out_specs=[col(gcols), state_spec],
        out_shape=[jax.ShapeDtypeStruct((n_batch * seq_len, d_inner), act_dtype),
                   jax.ShapeDtypeStruct(s0.shape, F32)],
        scratch_shapes=[pltpu.VMEM((n_state, gcols), F32)],
        input_output_aliases={n_in + a: 1 + a for a in range(n_alias)},
        compiler_params=_compiler_params(("parallel", "parallel", "arbitrary")),
        name=name,
    )(zs, xs, bm, cm, dt, s0, W['ssm_a_log'], W['ssm_dx'], W['ssm_gnorm'], head_to_cols, *prev)


def _trunk(x, p, even_states, odd_states, fresh, W, cfg, tag):
    n_batch, seq_len, d = x.shape
    m = n_batch * seq_len
    depth = p.shape[0]
    nseq, t_len, n_sub, tm_even, tm_odd, tm_post, act_dtype = cfg
    hgrn0, lru_h0, lru_conv0 = even_states
    ssm0, ssm_conv0 = odd_states
    x = x.reshape(m, d)
    p = p.reshape(depth, m, p.shape[-1])
    mixer_args = dict(n_batch=n_batch, seq_len=seq_len, nseq=nseq, t_len=t_len, n_sub=n_sub, act_dtype=act_dtype)
    even_out, even_conv, odd_out, odd_conv = (), (), (), ()
    for i in range(depth):
        j = i // 2
        if i % 2 == 0:
            *segs, conv = _proj_even(x, lru_conv0, even_conv, W, i, j, seq_len=seq_len, tm=tm_even,
                                     v_dtype=act_dtype, name=f"{tag}_proj{i}")
            even_conv = (conv,)
            act_a, act_b, *even_out = _even_mixer(segs, (hgrn0, lru_h0), even_out, W, j, fresh=fresh,
                                                  name=f"{tag}_even{i}", **mixer_args)
            acts, wo = (act_a, act_b), W['w_even_out']
        else:
            *segs, conv = _proj_odd(x, ssm_conv0, odd_conv, W, i, j, seq_len=seq_len, tm=tm_odd,
                                    bc_dtype=act_dtype, name=f"{tag}_proj{i}")
            odd_conv = (conv,)
            act, *odd_out = _odd_mixer(segs, ssm0, odd_out, W, j, n_heads=W['n_heads_c'],
                                       name=f"{tag}_odd{i}", **mixer_args)
            acts, wo = (act,), W['ssm_out']
        x = _post_mixer(x, acts, p, wo, j, W, i, i == depth - 1, tm_post, f"{tag}_post{i}")
    return x.reshape(n_batch, seq_len, d), even_out, even_conv[0], odd_out[0], odd_conv[0]


def kernel(x_prompt, x_sample, state_hgrn, state_lru_h, state_lru_conv, state_ssm, state_ssm_conv, p_prompt, p_sample, g_mix, g_ffn, g_ple, g_final, w_even_in, hgrn_lb, hgrn_gnorm, lru_conv_w, lru_conv_b, lru_wa, lru_ba, lru_wx, lru_bx, lru_lam, w_even_out, ssm_in, ssm_conv_w, ssm_conv_b, ssm_dt_bias, ssm_a_log, ssm_d, ssm_gnorm, ssm_out, ffn_w1, ffn_w3, ffn_w2, ple_up, ple_gate):
    n_even = state_hgrn.shape[0]
    n_odd, _, n_heads_c, p_c, n_c = state_ssm.shape
    wid = state_lru_h.shape[-1]
    d = x_prompt.shape[-1]
    d_inner = n_heads_c * p_c

    lb = jnp.cumsum(jax.nn.softmax(hgrn_lb.astype(F32), axis=0), axis=0)
    lb = lb - lb[0]
    lbp = jnp.stack([jnp.log(lb), jnp.log1p(-lb), 1.0 - lb], axis=1)
    n_main = ssm_in.shape[-1] - n_heads_c
    head_pad = -(-n_heads_c // LANES) * LANES
    pad_h = lambda a: jnp.pad(a.astype(F32), ((0, 0), (0, head_pad - n_heads_c))).reshape(n_odd, 1, head_pad)
    vec = lambda a: a.astype(F32).reshape(a.shape[0], 1, -1)
    W = dict(
        g_mix=vec(g_mix), g_ffn=vec(g_ffn), g_ple=vec(g_ple), g_final=g_final.reshape(1, d),
        lbp=lbp, hgrn_gnorm=vec(hgrn_gnorm), lru_conv_w=lru_conv_w, lru_conv_b=vec(lru_conv_b),
        lru_ba=vec(lru_ba), lru_bx=vec(lru_bx), lru_lam=vec(lru_lam),
        ssm_conv_w=ssm_conv_w, ssm_conv_b=vec(ssm_conv_b), ssm_gnorm=vec(ssm_gnorm),
        ssm_dt_bias=pad_h(ssm_dt_bias), ssm_a_log=pad_h(ssm_a_log),
        ssm_dx=jnp.repeat(ssm_d.astype(F32), p_c, axis=1).reshape(n_odd, 1, d_inner),
        w_even_in=w_even_in.astype(BF16), lru_wa=lru_wa.astype(BF16), lru_wx=lru_wx.astype(BF16),
        w_even_out=w_even_out.astype(BF16),
        ssm_in=ssm_in[..., :n_main].astype(BF16),
        ssm_in_dt=jnp.pad(ssm_in[..., n_main:].astype(BF16), ((0, 0), (0, 0), (0, head_pad - n_heads_c))),
        ssm_out=ssm_out.astype(BF16), ffn_w1=ffn_w1.astype(BF16), ffn_w3=ffn_w3.astype(BF16),
        ffn_w2=ffn_w2.astype(BF16), ple_up=ple_up.astype(BF16), ple_gate=ple_gate.astype(BF16),
        n_heads_c=n_heads_c)

    def run(x, p, hgrn, lru_h, lru_conv, ssm, ssm_conv, fresh, cfg, tag):
        nb = x.shape[0]
        even_states = (hgrn, lru_h.reshape(n_even, nb, 1, wid), lru_conv)
        odd_states = (ssm.reshape(n_odd, nb, d_inner, n_c), ssm_conv)
        y, (hg, lh), lc, ss, sc = _trunk(x, p, even_states, odd_states, fresh, W, cfg, tag)
        return y, hg, lh.reshape(n_even, nb, wid), lc, ss.reshape(n_odd, nb, n_heads_c, p_c, n_c), sc

    bp = x_prompt.shape[0]
    zeros = lambda ref: jnp.zeros((ref.shape[0], bp) + ref.shape[2:], F32)
    cfg_prompt = (1, 128, 16, 1024, 512, 512, BF16)
    cfg_sample = (16, x_sample.shape[1], 1, 512, 512, 512, F32)
    y_p, hg_p, lh_p, lc_p, ss_p, sc_p = run(
        x_prompt, p_prompt, zeros(state_hgrn), zeros(state_lru_h), zeros(state_lru_conv), zeros(state_ssm),
        zeros(state_ssm_conv), True, cfg_prompt, "prompt")
    y_s, hg_s, lh_s, lc_s, ss_s, sc_s = run(
        x_sample, p_sample, state_hgrn, state_lru_h, state_lru_conv, state_ssm, state_ssm_conv, False,
        cfg_sample, "sample")
    return (y_p, y_s, hg_p, hg_s, lh_p, lh_s, lc_p, lc_s, ss_p, ss_s, sc_p, sc_s)
```

```python
import functools
import math

import jax
import jax.numpy as jnp
from jax import lax
from jax.experimental import pallas as pl
from jax.experimental.pallas import tpu as pltpu

F32 = jnp.float32
BF16 = jnp.bfloat16
EPS = 1e-6
LRU_C = 8.0
CONV_W = 4
HIST = CONV_W - 1
SUBLANES = 8
LANES = 128
INT_BITS = 32
LOG2_E = math.log2(math.e)
VMEM_LIMIT = 56 * 1024 * 1024
STATE_STREAMS = 4
POST_ROWS = 256


def _dot(a, b):
    return jnp.dot(a, b, preferred_element_type=F32)


def _dot_tn(a, b):
    return lax.dot_general(a, b, (((0,), (0,)), ((), ())), preferred_element_type=F32)


def _dot_nt(a, b):
    return lax.dot_general(a, b, (((1,), (1,)), ((), ())), preferred_element_type=F32)


def _split_cat(x):
    hi, lo = _split_rows(x)
    return jnp.concatenate([hi, lo], axis=1)


def _prefix_sum(tri16, x):
    return _dot(jnp.concatenate([tri16, tri16], axis=1), jnp.concatenate(_split_rows(x), axis=0))


def _split_rows(x):
    hi = x.astype(BF16)
    lo = (x - hi.astype(F32)).astype(BF16)
    return [hi, lo]


def _rms(x, g):
    return x * lax.rsqrt(jnp.mean(x * x, axis=-1, keepdims=True) + EPS) * g


def _sigmoid(x):
    return 0.5 * jnp.tanh(0.5 * x) + 0.5


def _silu(x):
    half = 0.5 * x
    return half * jnp.tanh(half) + half


def _softplus(x):
    return jnp.maximum(x, 0.0) + jnp.log1p(jnp.exp(-jnp.abs(x)))


def _log_sigmoid(x):
    return jnp.minimum(x, 0.0) - jnp.log(1.0 + jnp.exp(-jnp.abs(x)))


def _logaddexp(a, b):
    return jnp.maximum(a, b) + jnp.log(1.0 + jnp.exp(-jnp.abs(a - b)))


def _gelu_tanh(x):
    return 0.5 * x * (1.0 + jnp.tanh(math.sqrt(2.0 / math.pi) * (x + 0.044715 * (x * x * x))))


def _seq_tri(rows, t_len):
    shift = t_len.bit_length() - 1
    r = lax.broadcasted_iota(jnp.int32, (rows, rows), 0)
    c = lax.broadcasted_iota(jnp.int32, (rows, rows), 1)
    return ((r >> shift) == (c >> shift)) & (c <= r)


def _seq_last(x, nseq, t_len):
    x3 = x.reshape(nseq, t_len, x.shape[-1])
    last = x3[:, t_len - 1:t_len, :]
    return jnp.broadcast_to(last, x3.shape).reshape(x.shape)


def _init_hist(hist_ref, c0_ref):
    nseq, _, ch = hist_ref.shape
    hist_ref[:, :SUBLANES - HIST, :] = jnp.zeros((nseq, SUBLANES - HIST, ch), F32)
    hist_ref[:, SUBLANES - HIST:, :] = c0_ref[...]


def _shift_rows(cur, hist, j, nseq, t_len):
    rows, ch = cur.shape
    row = lax.broadcasted_iota(jnp.int32, (nseq, SUBLANES, ch), 1)
    rolled = pltpu.roll(cur, j, 0).reshape(nseq, t_len, ch)
    hist_j = pltpu.roll(hist, (nseq * SUBLANES + j - SUBLANES) % (nseq * SUBLANES), 0)
    head = jnp.where(row < j, hist_j.reshape(nseq, SUBLANES, ch), rolled[:, 0:SUBLANES, :])
    shifted = head if t_len == SUBLANES else jnp.concatenate([head, rolled[:, SUBLANES:, :]], axis=1)
    return shifted.reshape(rows, ch)


def _conv_step(raw, hist_ref, w, bias, nseq, t_len):
    assert CONV_W == 4
    raw3 = raw.reshape(nseq, t_len, raw.shape[1])
    hist = hist_ref[...].reshape(nseq * SUBLANES, raw.shape[1])
    w0, w1, w2, w3 = (w[k:k + 1, :] for k in range(CONV_W))
    prev = _shift_rows(raw, hist, 1, nseq, t_len)
    pair = raw * w1 + prev * w0
    pair_hist = hist * w1 + pltpu.roll(hist, 1, 0) * w0
    out = raw * w3 + bias + prev * w2 + _shift_rows(pair, pair_hist, 2, nseq, t_len)
    hist_ref[...] = raw3[:, t_len - SUBLANES:, :]
    return out, raw3[:, t_len - HIST:, :]


def _resident(shape, layer=None):
    nd = len(shape)
    if layer is None:
        return pl.BlockSpec(shape, lambda *_: (0,) * nd, pipeline_mode=pl.Buffered(1))
    return pl.BlockSpec((None,) + tuple(shape), lambda *_: (layer,) + (0,) * nd, pipeline_mode=pl.Buffered(1))


def _compiler_params(semantics):
    return pltpu.CompilerParams(dimension_semantics=semantics, vmem_limit_bytes=VMEM_LIMIT)


def _conv_tiling(m, seq_len, tm):
    if tm >= seq_len:
        return tm // seq_len, seq_len, 1
    return 1, tm, seq_len // tm


def _proj_even_kernel(x_ref, g_ref, w_ref, c0_ref, cw_ref, cb_ref, lbp_ref, *rest, nseq, t_len, tiles_per_seq,
                      n_alias):
    q_ref, lf_ref, k_ref, v_ref, ga_ref, yb_ref, u_ref, co_ref, hist_ref = rest[n_alias:]
    wid = q_ref.shape[1]

    @pl.when(pl.program_id(0) % tiles_per_seq == 0)
    def _():
        _init_hist(hist_ref, c0_ref)

    lbp = lbp_ref[...]
    xn = _rms(x_ref[...], g_ref[...]).astype(BF16)
    seg = lambda s: _dot(xn, w_ref[:, s * wid:(s + 1) * wid])
    q_ref[...] = _silu(seg(0))
    fz = seg(1)
    lf_ref[...] = _logaddexp(lbp[0:1, :], lbp[1:2, :] + _log_sigmoid(fz))
    k_ref[...] = lbp[2:3, :] * _sigmoid(-fz)
    v_ref[...] = seg(2).astype(v_ref.dtype)
    ga_ref[...] = _silu(seg(3))
    yb_ref[...] = _gelu_tanh(seg(4))
    u, new_hist = _conv_step(seg(5), hist_ref, cw_ref[...], cb_ref[...], nseq, t_len)
    u_ref[...] = u
    co_ref[...] = new_hist


def _proj_even(x, c0, prev, W, layer, j, *, seq_len, tm, v_dtype, name):
    m, d = x.shape
    wid = c0.shape[-1]
    nseq, t_len, tiles_per_seq = _conv_tiling(m, seq_len, tm)
    row = lambda i: (i, 0)
    state_spec = pl.BlockSpec((None, nseq, HIST, wid), lambda i: (j, i // tiles_per_seq, 0, 0))
    seg = pl.BlockSpec((tm, wid), row)
    n_alias = len(prev)
    kern = functools.partial(_proj_even_kernel, nseq=nseq, t_len=t_len, tiles_per_seq=tiles_per_seq,
                             n_alias=n_alias)
    n_in = 7
    return pl.pallas_call(
        kern,
        grid=(m // tm,),
        in_specs=[pl.BlockSpec((tm, d), row), _resident((1, d), layer), _resident((d, 6 * wid), j), state_spec,
                  _resident((CONV_W, wid), j), _resident((1, wid), j), _resident((3, wid), j)]
                 + [pl.BlockSpec(memory_space=pl.ANY)] * n_alias,
        out_specs=[seg] * 7 + [state_spec],
        out_shape=[jax.ShapeDtypeStruct((m, wid), dt) for dt in (F32, F32, F32, v_dtype, F32, F32, F32)]
                  + [jax.ShapeDtypeStruct(c0.shape, F32)],
        scratch_shapes=[pltpu.VMEM((nseq, SUBLANES, wid), F32)],
        input_output_aliases={n_in + a: 7 + a for a in range(n_alias)},
        compiler_params=_compiler_params(("arbitrary",)),
        name=name,
    )(x, W['g_mix'], W['w_even_in'], c0, W['lru_conv_w'], W['lru_conv_b'], W['lbp'], *prev)


def _proj_odd_kernel(x_ref, g_ref, w_ref, wdt_ref, c0_ref, cw_ref, cb_ref, dtb_ref, *rest, nseq, t_len, tiles_per_seq,
                     n_alias):
    zs_ref, xs_ref, b_ref, c_ref, dt_ref, co_ref, hist_ref = rest[n_alias:]
    d_inner = zs_ref.shape[1]
    conv_dim = hist_ref.shape[2]
    bc_w = b_ref.shape[1]

    @pl.when(pl.program_id(0) % tiles_per_seq == 0)
    def _():
        _init_hist(hist_ref, c0_ref)

    xn = _rms(x_ref[...], g_ref[...]).astype(BF16)
    for c0 in range(0, d_inner, bc_w):
        zs_ref[:, c0:c0 + bc_w] = _silu(_dot(xn, w_ref[:, c0:c0 + bc_w]))
    for c0 in range(0, conv_dim, bc_w):
        cols = slice(c0, c0 + bc_w)
        conv, new_hist = _conv_step(_dot(xn, w_ref[:, d_inner + c0:d_inner + c0 + bc_w]), hist_ref.at[:, :, cols],
                                    cw_ref[:, cols], cb_ref[:, cols], nseq, t_len)
        co_ref[:, :, cols] = new_hist
        act = _silu(conv)
        if c0 < d_inner:
            xs_ref[:, cols] = act
        elif c0 == d_inner:
            b_ref[...] = act.astype(b_ref.dtype)
        else:
            c_ref[...] = act.astype(c_ref.dtype)
    dt_ref[...] = _softplus(_dot(xn, wdt_ref[...]) + dtb_ref[...])


def _proj_odd(x, c0, prev, W, layer, j, *, seq_len, tm, bc_dtype, name):
    m, d = x.shape
    conv_dim = c0.shape[-1]
    n_main = W['ssm_in'].shape[-1]
    head_lanes = W['ssm_dt_bias'].shape[-1]
    d_inner = n_main - conv_dim
    bc_w = (conv_dim - d_inner) // 2
    nseq, t_len, tiles_per_seq = _conv_tiling(m, seq_len, tm)
    row = lambda i: (i, 0)
    state_spec = pl.BlockSpec((None, nseq, HIST, conv_dim), lambda i: (j, i // tiles_per_seq, 0, 0))
    n_alias = len(prev)
    kern = functools.partial(_proj_odd_kernel, nseq=nseq, t_len=t_len, tiles_per_seq=tiles_per_seq,
                             n_alias=n_alias)
    n_in = 8
    return pl.pallas_call(
        kern,
        grid=(m // tm,),
        in_specs=[pl.BlockSpec((tm, d), row), _resident((1, d), layer), _resident((d, n_main), j),
                  _resident((d, head_lanes), j), state_spec,
                  _resident((CONV_W, conv_dim), j), _resident((1, conv_dim), j), _resident((1, head_lanes), j)]
                 + [pl.BlockSpec(memory_space=pl.ANY)] * n_alias,
        out_specs=[pl.BlockSpec((tm, d_inner), row), pl.BlockSpec((tm, d_inner), row),
                   pl.BlockSpec((tm, bc_w), row), pl.BlockSpec((tm, bc_w), row),
                   pl.BlockSpec((tm, head_lanes), row), state_spec],
        out_shape=[jax.ShapeDtypeStruct((m, d_inner), F32), jax.ShapeDtypeStruct((m, d_inner), F32),
                   jax.ShapeDtypeStruct((m, bc_w), bc_dtype), jax.ShapeDtypeStruct((m, bc_w), bc_dtype),
                   jax.ShapeDtypeStruct((m, head_lanes), F32), jax.ShapeDtypeStruct(c0.shape, F32)],
        scratch_shapes=[pltpu.VMEM((nseq, SUBLANES, conv_dim), F32)],
        input_output_aliases={n_in + a: 5 + a for a in range(n_alias)},
        compiler_params=_compiler_params(("arbitrary",)),
        name=name,
    )(x, W['g_mix'], W['ssm_in'], W['ssm_in_dt'], c0, W['ssm_conv_w'], W['ssm_conv_b'], W['ssm_dt_bias'], *prev)


def _post_kernel(*refs, n_act, final):
    x_ref, p_ref = refs[0], refs[1]
    act_refs = refs[2:2 + n_act]
    wo_refs = refs[2 + n_act:2 + 2 * n_act]
    gf_ref, w1_ref, w3_ref, w2_ref, wg_ref, wu_ref, gp_ref, gfin_ref, o_ref = refs[2 + 2 * n_act:]
    tm = x_ref.shape[0]
    rb = min(tm, POST_ROWS)
    for r0 in range(0, tm, rb):
        rs = slice(r0, r0 + rb)
        x = x_ref[rs, :]
        for act_ref, wo_ref in zip(act_refs, wo_refs):
            x = x + _dot(act_ref[rs, :].astype(BF16), wo_ref[...])
        xn = _rms(x, gf_ref[...]).astype(BF16)
        h = (_silu(_dot(xn, w1_ref[...])) * _dot(xn, w3_ref[...])).astype(BF16)
        x = x + _dot(h, w2_ref[...])
        gate = _sigmoid(_dot(x.astype(BF16), wg_ref[...]))
        emb = _dot(p_ref[rs, :].astype(BF16), wu_ref[...])
        x = x + _rms(gate * emb, gp_ref[...])
        if final:
            x = _rms(x, gfin_ref[...])
        o_ref[rs, :] = x


def _post_mixer(x, acts, p, wo, wo_layer, W, layer, final, tm, name):
    m, d = x.shape
    dp = p.shape[-1]
    dff = W['ffn_w1'].shape[-1]
    row = lambda i: (i, 0)
    act_specs, wo_specs, off = [], [], 0
    for a in acts:
        ka = a.shape[1]
        act_specs.append(pl.BlockSpec((tm, ka), row))
        wo_specs.append(pl.BlockSpec((None, ka, d), lambda i, blk=off // ka: (wo_layer, blk, 0),
                                     pipeline_mode=pl.Buffered(1)))
        off += ka
    return pl.pallas_call(
        functools.partial(_post_kernel, n_act=len(acts), final=final),
        grid=(m // tm,),
        in_specs=[pl.BlockSpec((tm, d), row), pl.BlockSpec((None, tm, dp), lambda i: (layer, i, 0))]
                 + act_specs + wo_specs + [
                  _resident((1, d), layer), _resident((d, dff), layer), _resident((d, dff), layer),
                  _resident((dff, d), layer), _resident((d, d), layer), _resident((dp, d), layer),
                  _resident((1, d), layer), _resident((1, d))],
        out_specs=pl.BlockSpec((tm, d), row),
        out_shape=jax.ShapeDtypeStruct((m, d), F32),
        compiler_params=_compiler_params(("parallel",)),
        name=name,
    )(x, p, *acts, *([wo] * len(acts)), W['g_ffn'], W['ffn_w1'], W['ffn_w3'], W['ffn_w2'], W['ple_gate'],
      W['ple_up'], W['g_ple'], W['g_final'])


def _even_kernel(q_ref, lf_ref, k_ref, v_ref, ga_ref, yb_ref, u_ref, s0_ref, h0_ref, gn_ref, wa_ref, ba_ref,
                 wx_ref, bx_ref, lam_ref, *rest, nseq, t_len, n_sub, fresh, n_alias):
    oa_ref, ob_ref, s_ref, h_ref = rest[n_alias:]
    rows = nseq * t_len
    wid = q_ref.shape[1]
    levels = t_len.bit_length() - 1
    tiles = t_len // SUBLANES
    step = pl.program_id(2)

    @pl.when(step == 0)
    def _():
        s_ref[...] = s0_ref[...]
        h_ref[...] = h0_ref[...]

    t_in = lax.broadcasted_iota(jnp.int32, (rows, wid), 0) & (t_len - 1)
    t_sub3 = lax.broadcasted_iota(jnp.int32, (rows // SUBLANES, SUBLANES, wid), 1)
    tri16 = _seq_tri(rows, t_len).astype(BF16)
    r_i = lax.broadcasted_iota(jnp.int32, (rows, rows), 0)
    c_i = lax.broadcasted_iota(jnp.int32, (rows, rows), 1)
    pair_level = jnp.where(c_i > r_i, -1, INT_BITS - lax.clz(r_i ^ c_i))
    level_masks = [pair_level == lvl for lvl in range(levels + 1)]
    soft_lam = _softplus(-lam_ref[...])

    for sub in range(n_sub):
        rsl = slice(sub * rows, (sub + 1) * rows)
        u = u_ref[rsl, :]
        u16 = u.astype(BF16)
        r = _sigmoid(_dot(u16, wa_ref[...]) + ba_ref[...])
        gi = _sigmoid(_dot(u16, wx_ref[...]) + bx_ref[...])
        log_a = (-LRU_C) * r * soft_lam
        a = jnp.exp(log_a)
        gain2 = -jnp.tanh(log_a) * (a * a + 1.0)
        mult = jnp.where(gain2 > 0.0, gain2 * lax.rsqrt(gain2), 0.0)
        if fresh and sub == 0:
            mult = jnp.where((t_in == 0) & (step == 0), 1.0, mult)
        bt = mult * (gi * u)
        a = a.reshape(rows // SUBLANES, SUBLANES, wid)
        bt = bt.reshape(rows // SUBLANES, SUBLANES, wid)
        shift = 1
        while shift < SUBLANES:
            valid = t_sub3 >= shift
            a_sh = pltpu.roll(a, shift, 1)
            b_sh = pltpu.roll(bt, shift, 1)
            bt = jnp.where(valid, a * b_sh + bt, bt)
            a = jnp.where(valid, a * a_sh, a)
            shift *= 2
        a4 = a.reshape(nseq, tiles, SUBLANES, wid)
        b4 = bt.reshape(nseq, tiles, SUBLANES, wid)
        carry = h_ref[...]
        h_tiles = []
        for tile in range(tiles):
            h_k = a4[:, tile] * carry + b4[:, tile]
            carry = h_k[:, SUBLANES - 1:SUBLANES, :]
            h_tiles.append(h_k)
        h_ref[...] = carry
        hseq = (h_tiles[0] if tiles == 1 else jnp.concatenate(h_tiles, axis=1)).reshape(rows, wid)
        ob_ref[rsl, :] = (yb_ref[rsl, :] * hseq).astype(ob_ref.dtype)

        logf = lf_ref[rsl, :]
        k = k_ref[rsl, :]
        q = q_ref[rsl, :]
        v16 = v_ref[rsl, :].astype(BF16)
        b = _prefix_sum(tri16, logf)
        b_last = _seq_last(b, nseq, t_len)
        qe16 = (q * jnp.exp(b)).astype(BF16)
        kd16 = (k * jnp.exp(b_last - b)).astype(BF16)
        q16 = q.astype(BF16)
        k16 = k.astype(BF16)

        att = jnp.where(level_masks[0], _dot_nt(q16, k16), 0.0)
        for lvl in range(1, levels + 1):
            blk = 1 << lvl
            if lvl == 1:
                w = jnp.exp(jnp.where((t_in & 1) == 1, logf, 0.0))
            elif lvl == 2:
                logf3 = logf.reshape(rows // SUBLANES, SUBLANES, wid)
                nxt = pltpu.roll(logf3, SUBLANES - 1, 1).reshape(rows, wid)
                prv = pltpu.roll(logf3, 1, 1).reshape(rows, wid)
                pos = t_in & 3
                w = jnp.exp(jnp.where(pos == 0, nxt, jnp.where(pos == 1, 0.0, jnp.where(pos == 2, logf, logf + prv))))
            else:
                b3 = b.reshape(rows // blk, blk, wid)
                mid = jnp.broadcast_to(b3[:, blk // 2 - 1:blk // 2, :], b3.shape).reshape(rows, wid)
                w = jnp.exp2(jnp.abs(b - mid) * (-LOG2_E))
            w16 = w.astype(BF16)
            att = jnp.where(level_masks[lvl], _dot_nt(q16 * w16, k16 * w16), att)

        o = _dot(att.astype(BF16), v16)
        inter = []
        for i in range(nseq):
            rs = slice(i * t_len, (i + 1) * t_len)
            s_old = s_ref[i]
            inter.append(_dot(qe16[rs, :], s_old.astype(BF16)))
            e_last = jnp.exp(b[(i + 1) * t_len - 1:(i + 1) * t_len, :])
            scale = jnp.transpose(jnp.broadcast_to(e_last, (wid, wid)))
            s_ref[i] = scale * s_old + _dot_tn(kd16[rs, :], v16[rs, :])
        o = o + (inter[0] if nseq == 1 else jnp.concatenate(inter, axis=0))
        o = o * lax.rsqrt(jnp.mean(o * o, axis=-1, keepdims=True) + EPS)
        oa_ref[rsl, :] = (o * gn_ref[...] * ga_ref[rsl, :]).astype(oa_ref.dtype)


def _even_mixer(segs, states, prev, W, j, *, n_batch, seq_len, nseq, t_len, n_sub, fresh, act_dtype, name):
    s0, h0 = states
    n_heads, dk = s0.shape[2], s0.shape[3]
    wid = n_heads * dk
    rows_step = nseq * t_len * n_sub
    n_steps = seq_len // (t_len * n_sub)
    n_alias = len(prev)
    kern = functools.partial(_even_kernel, nseq=nseq, t_len=t_len, n_sub=n_sub, fresh=fresh, n_alias=n_alias)
    col = pl.BlockSpec((rows_step, dk), lambda b, h, s: (b * n_steps + s, h))
    par = pl.BlockSpec((None, 1, dk), lambda b, h, s: (j, 0, h))
    gate_w = pl.BlockSpec((None, None, dk, dk), lambda b, h, s: (j, h, 0, 0))
    state_specs = [pl.BlockSpec((None, nseq, None, dk, dk), lambda b, h, s: (j, b, h, 0, 0)),
                   pl.BlockSpec((None, nseq, 1, dk), lambda b, h, s: (j, b, 0, h))]
    n_in = 15
    return pl.pallas_call(
        kern,
        grid=(n_batch // nseq, n_heads, n_steps),
        in_specs=[col] * 7 + state_specs + [par, gate_w, par, gate_w, par, par]
                 + [pl.BlockSpec(memory_space=pl.ANY)] * n_alias,
        out_specs=[col, col] + state_specs,
        out_shape=[jax.ShapeDtypeStruct((n_batch * seq_len, wid), act_dtype),
                   jax.ShapeDtypeStruct((n_batch * seq_len, wid), act_dtype),
                   jax.ShapeDtypeStruct(s0.shape, F32),
                   jax.ShapeDtypeStruct(h0.shape, F32)],
        input_output_aliases={n_in + a: 2 + a for a in range(n_alias)},
        compiler_params=_compiler_params(("parallel", "parallel", "arbitrary")),
        name=name,
    )(*segs, s0, h0, W['hgrn_gnorm'], W['lru_wa'], W['lru_ba'], W['lru_wx'], W['lru_bx'], W['lru_lam'], *prev)


def _odd_kernel(zs_ref, xs_ref, bm_ref, cm_ref, dt_ref, *rest, nseq, t_len, n_sub, n_steps, hpg, n_streams, n_alias):
    s0_refs = rest[:n_streams]
    alog_ref, dx_ref, gn_ref, ecols_ref = rest[n_streams:n_streams + 4]
    act_ref, s_ref, st_ref = rest[n_streams + 4 + n_alias:]
    carry_t = n_sub * n_steps > 1
    rows = nseq * t_len
    gcols = xs_ref.shape[1]
    p_dim = gcols // hpg
    n_state = bm_ref.shape[1]
    lanes = 2 * p_dim
    group = pl.program_id(1)
    step = pl.program_id(2)

    @pl.when(step == 0)
    def _():
        part = gcols // n_streams
        for k, s0_ref in enumerate(s0_refs):
            if carry_t:
                st_ref[:, k * part:(k + 1) * part] = jnp.transpose(s0_ref[0])
            else:
                s_ref[:, k * part:(k + 1) * part, :] = s0_ref[...]

    tri = _seq_tri(rows, t_len)
    tri16 = tri.astype(BF16)
    low = lax.broadcasted_iota(jnp.int32, (rows, lanes), 1) < p_dim
    head_lanes = dt_ref.shape[1]
    to_front = lax.rem(head_lanes - group * hpg, head_lanes)
    neg_a = -jnp.exp(alog_ref[...])

    for sub in range(n_sub):
        rsl = slice(sub * rows, (sub + 1) * rows)
        xs = xs_ref[rsl, :]
        b16 = bm_ref[rsl, :].astype(BF16)
        c16 = cm_ref[rsl, :].astype(BF16)
        dt_all = dt_ref[rsl, :]
        dt = pltpu.roll(dt_all, to_front, 1)
        la = pltpu.roll(dt_all * neg_a, to_front, 1)
        cum = _prefix_sum(tri16, la)
        cum_last = _seq_last(cum, nseq, t_len)
        cum_t = jnp.transpose(cum)
        dt_t = jnp.transpose(dt)
        e_cols = _dot(_split_cat(jnp.exp(cum)), ecols_ref[...])
        w_cols = _dot(_split_cat(jnp.exp(cum_last - cum) * dt), ecols_ref[...])

        cb_g = _dot_nt(c16, b16)
        if carry_t:
            y_int = _dot(c16, st_ref[...].astype(BF16))
        else:
            y_int = []
            for i in range(nseq):
                rs = slice(i * t_len, (i + 1) * t_len)
                y_int.append(_dot_nt(c16[rs, :], s_ref[i].astype(BF16)))
            y_int = y_int[0] if nseq == 1 else jnp.concatenate(y_int, axis=0)
        x_upd = (xs * w_cols).astype(BF16)
        y_cols = []
        for hp in range(hpg // 2):
            cols = slice(2 * hp * p_dim, (2 * hp + 2) * p_dim)
            x_pair = xs[:, cols]
            m_pair, x_blocks = [], []
            for half in range(2):
                h = 2 * hp + half
                cum_row = jnp.broadcast_to(cum_t[h:h + 1, :], (rows, rows))
                seg = jnp.transpose(cum_row) - cum_row
                decay = jnp.exp(jnp.where(tri, seg, -jnp.inf))
                m_pair.append((cb_g * (decay * jnp.broadcast_to(dt_t[h:h + 1, :], (rows, rows)))).astype(BF16))
                x_blocks.append(jnp.where(low if half == 0 else ~low, x_pair, 0.0).astype(BF16))
            y_cols.append(e_cols[:, cols] * y_int[:, cols] + dx_ref[:, cols] * x_pair
                          + _dot(jnp.concatenate(m_pair, axis=1), jnp.concatenate(x_blocks, axis=0)))
        if carry_t:
            st_ref[...] = st_ref[...] * e_cols[rows - 1:rows, :] + _dot_tn(b16, x_upd)
        for i in range(0 if carry_t else nseq):
            rs = slice(i * t_len, (i + 1) * t_len)
            upd = _dot_tn(x_upd[rs, :], b16[rs, :])
            last_t = jnp.transpose(jnp.broadcast_to(cum[(i + 1) * t_len - 1:(i + 1) * t_len, :],
                                                    (head_lanes, head_lanes)))
            for h in range(hpg):
                hr = slice(h * p_dim, (h + 1) * p_dim)
                dec = jnp.exp(jnp.broadcast_to(last_t[h:h + 1, 0:n_state], (p_dim, n_state)))
                s_ref[i, hr, :] = dec * s_ref[i, hr, :] + upd[hr, :]

        y = jnp.concatenate(y_cols, axis=1) * zs_ref[rsl, :]
        y = y * lax.rsqrt(jnp.mean(y * y, axis=-1, keepdims=True) + EPS)
        act_ref[rsl, :] = (y * gn_ref[...]).astype(act_ref.dtype)

    if carry_t:
        @pl.when(step == n_steps - 1)
        def _():
            s_ref[0] = jnp.transpose(st_ref[...])


def _odd_mixer(segs, s0, prev, W, j, *, n_batch, seq_len, nseq, t_len, n_sub, n_heads, act_dtype, name):
    zs, xs, bm, cm, dt = segs
    d_inner, n_state = s0.shape[2], s0.shape[3]
    n_groups = bm.shape[1] // n_state
    hpg = n_heads // n_groups
    gcols = d_inner // n_groups
    head_lanes = dt.shape[1]
    rows = nseq * t_len
    rows_step = rows * n_sub
    n_steps = seq_len // (t_len * n_sub)
    n_alias = len(prev)
    assert nseq == 1 or n_sub * n_steps == 1, "several sequences per step are swept in one sub-chunk"
    n_streams = STATE_STREAMS if nseq > 1 else 1
    kern = functools.partial(_odd_kernel, nseq=nseq, t_len=t_len, n_sub=n_sub, n_steps=n_steps, hpg=hpg,
                             n_streams=n_streams, n_alias=n_alias)
    part = gcols // n_streams
    state_in_specs = [pl.BlockSpec((None, nseq, part, n_state), lambda b, g, s, k=k: (j, b, g * n_streams + k, 0))
                      for k in range(n_streams)]
    col = lambda width: pl.BlockSpec((rows_step, width), lambda b, g, s: (b * n_steps + s, g))
    par = pl.BlockSpec((None, 1, gcols), lambda b, g, s: (j, 0, g))
    state_spec = pl.BlockSpec((None, nseq, gcols, n_state), lambda b, g, s: (j, b, g, 0))
    head_to_cols = jnp.tile(jnp.repeat(jnp.eye(head_lanes, hpg, dtype=BF16), gcols // hpg, axis=1), (2, 1))
    n_in = 9 + n_streams
    return pl.pallas_call(
        kern,
        grid=(n_batch // nseq, n_groups, n_steps),
        in_specs=[col(gcols), col(gcols), col(n_state), col(n_state),
                  pl.BlockSpec((rows_step, head_lanes), lambda b, g, s: (b * n_steps + s, 0))] + state_in_specs + [
                  _resident((1, head_lanes), j), par, par, _resident(head_to_cols.shape)]
                 + [pl.BlockSpec(memory_space=pl.ANY)] * n_alias,
        out_specs=[col(gcols), state_spec],
        out_shape=[jax.ShapeDtypeStruct((n_batch * seq_len, d_inner), act_dtype),
                   jax.ShapeDtypeStruct(s0.shape, F32)],
        scratch_shapes=[pltpu.VMEM((n_state, gcols), F32)],
        input_output_aliases={n_in + a: 1 + a for a in range(n_alias)},
        compiler_params=_compiler_params(("parallel", "parallel", "arbitrary")),
        name=name,
    )(zs, xs, bm, cm, dt, *([s0] * n_streams), W['ssm_a_log'], W['ssm_dx'], W['ssm_gnorm'], head_to_cols, *prev)


def _trunk(x, p, even_states, odd_states, fresh, W, cfg, tag):
    n_batch, seq_len, d = x.shape
    m = n_batch * seq_len
    depth = p.shape[0]
    nseq, t_len, n_sub, tm_even, tm_odd, tm_post, act_dtype = cfg
    hgrn0, lru_h0, lru_conv0 = even_states
    ssm0, ssm_conv0 = odd_states
    x = x.reshape(m, d)
    p = p.reshape(depth, m, p.shape[-1])
    mixer_args = dict(n_batch=n_batch, seq_len=seq_len, nseq=nseq, t_len=t_len, n_sub=n_sub, act_dtype=act_dtype)
    even_out, even_conv, odd_out, odd_conv = (), (), (), ()
    for i in range(depth):
        j = i // 2
        if i % 2 == 0:
            *segs, conv = _proj_even(x, lru_conv0, even_conv, W, i, j, seq_len=seq_len, tm=tm_even,
                                     v_dtype=act_dtype, name=f"{tag}_proj{i}")
            even_conv = (conv,)
            act_a, act_b, *even_out = _even_mixer(segs, (hgrn0, lru_h0), even_out, W, j, fresh=fresh,
                                                  name=f"{tag}_even{i}", **mixer_args)
            acts, wo = (act_a, act_b), W['w_even_out']
        else:
            *segs, conv = _proj_odd(x, ssm_conv0, odd_conv, W, i, j, seq_len=seq_len, tm=tm_odd,
                                    bc_dtype=act_dtype, name=f"{tag}_proj{i}")
            odd_conv = (conv,)
            act, *odd_out = _odd_mixer(segs, ssm0, odd_out, W, j, n_heads=W['n_heads_c'],
                                       name=f"{tag}_odd{i}", **mixer_args)
            acts, wo = (act,), W['ssm_out']
        x = _post_mixer(x, acts, p, wo, j, W, i, i == depth - 1, tm_post, f"{tag}_post{i}")
    return x.reshape(n_batch, seq_len, d), even_out, even_conv[0], odd_out[0], odd_conv[0]


def kernel(x_prompt, x_sample, state_hgrn, state_lru_h, state_lru_conv, state_ssm, state_ssm_conv, p_prompt, p_sample, g_mix, g_ffn, g_ple, g_final, w_even_in, hgrn_lb, hgrn_gnorm, lru_conv_w, lru_conv_b, lru_wa, lru_ba, lru_wx, lru_bx, lru_lam, w_even_out, ssm_in, ssm_conv_w, ssm_conv_b, ssm_dt_bias, ssm_a_log, ssm_d, ssm_gnorm, ssm_out, ffn_w1, ffn_w3, ffn_w2, ple_up, ple_gate):
    n_even = state_hgrn.shape[0]
    n_odd, _, n_heads_c, p_c, n_c = state_ssm.shape
    wid = state_lru_h.shape[-1]
    d = x_prompt.shape[-1]
    d_inner = n_heads_c * p_c

    lb = jnp.cumsum(jax.nn.softmax(hgrn_lb.astype(F32), axis=0), axis=0)
    lb = lb - lb[0]
    lbp = jnp.stack([jnp.log(lb), jnp.log1p(-lb), 1.0 - lb], axis=1)
    n_main = ssm_in.shape[-1] - n_heads_c
    head_pad = -(-n_heads_c // LANES) * LANES
    pad_h = lambda a: jnp.pad(a.astype(F32), ((0, 0), (0, head_pad - n_heads_c))).reshape(n_odd, 1, head_pad)
    vec = lambda a: a.astype(F32).reshape(a.shape[0], 1, -1)
    W = dict(
        g_mix=vec(g_mix), g_ffn=vec(g_ffn), g_ple=vec(g_ple), g_final=g_final.reshape(1, d),
        lbp=lbp, hgrn_gnorm=vec(hgrn_gnorm), lru_conv_w=lru_conv_w, lru_conv_b=vec(lru_conv_b),
        lru_ba=vec(lru_ba), lru_bx=vec(lru_bx), lru_lam=vec(lru_lam),
        ssm_conv_w=ssm_conv_w, ssm_conv_b=vec(ssm_conv_b), ssm_gnorm=vec(ssm_gnorm),
        ssm_dt_bias=pad_h(ssm_dt_bias), ssm_a_log=pad_h(ssm_a_log),
        ssm_dx=jnp.repeat(ssm_d.astype(F32), p_c, axis=1).reshape(n_odd, 1, d_inner),
        w_even_in=w_even_in.astype(BF16), lru_wa=lru_wa.astype(BF16), lru_wx=lru_wx.astype(BF16),
        w_even_out=w_even_out.astype(BF16),
        ssm_in=ssm_in[..., :n_main].astype(BF16),
        ssm_in_dt=jnp.pad(ssm_in[..., n_main:].astype(BF16), ((0, 0), (0, 0), (0, head_pad - n_heads_c))),
        ssm_out=ssm_out.astype(BF16), ffn_w1=ffn_w1.astype(BF16), ffn_w3=ffn_w3.astype(BF16),
        ffn_w2=ffn_w2.astype(BF16), ple_up=ple_up.astype(BF16), ple_gate=ple_gate.astype(BF16),
        n_heads_c=n_heads_c)

    def run(x, p, hgrn, lru_h, lru_conv, ssm, ssm_conv, fresh, cfg, tag):
        nb = x.shape[0]
        even_states = (hgrn, lru_h.reshape(n_even, nb, 1, wid), lru_conv)
        odd_states = (ssm.reshape(n_odd, nb, d_inner, n_c), ssm_conv)
        y, (hg, lh), lc, ss, sc = _trunk(x, p, even_states, odd_states, fresh, W, cfg, tag)
        return y, hg, lh.reshape(n_even, nb, wid), lc, ss.reshape(n_odd, nb, n_heads_c, p_c, n_c), sc

    bp = x_prompt.shape[0]
    zeros = lambda ref: jnp.zeros((ref.shape[0], bp) + ref.shape[2:], F32)
    cfg_prompt = (1, 128, 16, 1024, 512, 512, BF16)
    cfg_sample = (16, x_sample.shape[1], 1, 512, 512, 512, F32)
    y_p, hg_p, lh_p, lc_p, ss_p, sc_p = run(
        x_prompt, p_prompt, zeros(state_hgrn), zeros(state_lru_h), zeros(state_lru_conv), zeros(state_ssm),
        zeros(state_ssm_conv), True, cfg_prompt, "prompt")
    y_s, hg_s, lh_s, lc_s, ss_s, sc_s = run(
        x_sample, p_sample, state_hgrn, state_lru_h, state_lru_conv, state_ssm, state_ssm_conv, False,
        cfg_sample, "sample")
    return (y_p, y_s, hg_p, hg_s, lh_p, lh_s, lc_p, lc_s, ss_p, ss_s, sc_p, sc_s)
```
